```python
import math
import jax, jax.numpy as jnp
from jax import lax
import numpy as np

D_MODEL = 1024
BATCH = 8
SEQ = 8192
DEPTH = 4

N_A = DEPTH // 2
N_B = DEPTH - N_A
ALPHA = (2.0 * DEPTH) ** 0.25
BETA_INIT = (8.0 * DEPTH) ** -0.25
LN_EPS = 1e-5
D_FF = 2816
GDN_HEAD_DIM = 128
GDN_HEADS = D_MODEL // GDN_HEAD_DIM
GDN_WIDTH = GDN_HEADS * GDN_HEAD_DIM
GDN_CONV = 4
GDN_CHUNK = 64
GDN_IN = 4 * GDN_WIDTH + 2 * GDN_HEADS
DIFF_HEAD_DIM = 64
DIFF_HEADS = D_MODEL // (2 * DIFF_HEAD_DIM)
DIFF_WIDTH = DIFF_HEADS * 2 * DIFF_HEAD_DIM
Q_BLOCK = 128
NORM_EPS = 1e-5

kernel_name = "yoco_gdn_diffattn_macaron_deepnorm"


def layer_norm(x, g, b):
    xf = x.astype(jnp.float32)
    mu = jnp.mean(xf, axis=-1, keepdims=True)
    xc = xf - mu
    var = jnp.mean(xc * xc, axis=-1, keepdims=True)
    return (xc * lax.rsqrt(var + LN_EPS) * g + b).astype(x.dtype)


def rms_norm(x, g, eps):
    xf = x.astype(jnp.float32)
    return xf * lax.rsqrt(jnp.mean(xf * xf, axis=-1, keepdims=True) + eps) * g


def l2_norm(x):
    return x * lax.rsqrt(jnp.sum(x * x, axis=-1, keepdims=True) + 1e-6)


def swiglu(x, w_in, w_out):
    gate, up = jnp.split(x @ w_in, 2, axis=-1)
    return (jax.nn.silu(gate) * up) @ w_out


def causal_depthwise_conv(x, w):
    return lax.conv_general_dilated(
        x, w[:, None, :], window_strides=(1,), padding=[(w.shape[0] - 1, 0)],
        dimension_numbers=("NWC", "WIO", "NWC"), feature_group_count=x.shape[-1])


def gated_delta_rule_chunked(q, k, v, g, beta):
    Bsz, T, H, DK = q.shape
    DV = v.shape[-1]
    N = T // GDN_CHUNK

    def to_chunks(t):
        return t.reshape(Bsz, N, GDN_CHUNK, H, -1).transpose(0, 3, 1, 2, 4)

    q, k, v = to_chunks(q), to_chunks(k), to_chunks(v)
    g = g.reshape(Bsz, N, GDN_CHUNK, H).transpose(0, 3, 1, 2)
    beta = beta.reshape(Bsz, N, GDN_CHUNK, H).transpose(0, 3, 1, 2)
    gc = jnp.cumsum(g, axis=-1)
    causal = jnp.tril(jnp.ones((GDN_CHUNK, GDN_CHUNK), dtype=bool))
    strict = jnp.tril(jnp.ones((GDN_CHUNK, GDN_CHUNK), dtype=bool), k=-1)
    decay_mat = jnp.exp(jnp.where(causal, gc[..., :, None] - gc[..., None, :], -jnp.inf))
    kb = k * beta[..., None]
    a_mat = jnp.where(strict, jnp.einsum("bhncd,bhnsd->bhncs", kb, k) * decay_mat, 0.0)
    eye = jnp.eye(GDN_CHUNK, dtype=jnp.float32)
    rhs = jnp.concatenate([kb * jnp.exp(gc)[..., None], v * beta[..., None]], axis=-1)
    wu = lax.linalg.triangular_solve(eye + a_mat, rhs, left_side=True, lower=True, unit_diagonal=True)
    w, u = wu[..., :DK], wu[..., DK:]
    attn = jnp.einsum("bhncd,bhnsd->bhncs", q, k) * decay_mat
    q_dec = q * jnp.exp(gc)[..., None]
    gl = gc[..., -1:]
    k_dec = k * jnp.exp(gl - gc)[..., None]
    chunk_decay = jnp.exp(gl[..., 0])

    def step(S, inp):
        w_n, u_n, qd_n, kd_n, at_n, cd_n = inp
        v_new = u_n - jnp.einsum("bhck,bhkv->bhcv", w_n, S)
        o = jnp.einsum("bhck,bhkv->bhcv", qd_n, S) + jnp.einsum("bhcs,bhsv->bhcv", at_n, v_new)
        S = S * cd_n[..., None, None] + jnp.einsum("bhck,bhcv->bhkv", kd_n, v_new)
        return S, o

    xs = tuple(jnp.moveaxis(t, 2, 0) for t in (w, u, q_dec, k_dec, attn, chunk_decay))
    S0 = jnp.zeros((Bsz, H, DK, DV), jnp.float32)
    _, o = lax.scan(step, S0, xs)
    return o.transpose(1, 0, 3, 2, 4).reshape(Bsz, T, H, DV)


def gated_deltanet(x, w_in, conv_w, a_log, dt_bias, norm_g, w_out):
    Bsz, T, _ = x.shape
    proj = x @ w_in
    qkv = jax.nn.silu(causal_depthwise_conv(proj[..., :3 * GDN_WIDTH], conv_w))
    z = proj[..., 3 * GDN_WIDTH:4 * GDN_WIDTH]
    b = proj[..., 4 * GDN_WIDTH:4 * GDN_WIDTH + GDN_HEADS]
    a = proj[..., 4 * GDN_WIDTH + GDN_HEADS:]
    q, k, v = jnp.split(qkv.astype(jnp.float32), 3, axis=-1)
    shp = (Bsz, T, GDN_HEADS, GDN_HEAD_DIM)
    q = l2_norm(q.reshape(shp)) * (GDN_HEAD_DIM ** -0.5)
    k = l2_norm(k.reshape(shp))
    v = v.reshape(shp)
    beta = jax.nn.sigmoid(b.astype(jnp.float32))
    g = -jnp.exp(a_log.astype(jnp.float32)) * jax.nn.softplus(a.astype(jnp.float32) + dt_bias.astype(jnp.float32))
    o = gated_delta_rule_chunked(q, k, v, g, beta)
    o = rms_norm(o, norm_g.astype(jnp.float32), NORM_EPS) * jax.nn.silu(z.astype(jnp.float32).reshape(shp))
    return o.reshape(Bsz, T, GDN_WIDTH).astype(x.dtype) @ w_out


def diff_softmax_attention(q, k, v, lam):
    Bsz, T, H = q.shape[:3]
    nb = T // Q_BLOCK
    qb = q.reshape(Bsz, nb, Q_BLOCK, H, 2, DIFF_HEAD_DIM).transpose(1, 0, 2, 3, 4, 5)
    kpos = jnp.arange(T)

    def one_block(args):
        q_blk, start = args
        s = jnp.einsum("bqhcd,bkhcd->bhcqk", q_blk, k, preferred_element_type=jnp.float32)
        qpos = start + jnp.arange(Q_BLOCK)
        s = jnp.where((kpos[None, :] <= qpos[:, None])[None, None, None], s, -jnp.inf)
        p = jax.nn.softmax(s, axis=-1)
        att = p[:, :, 0] - lam * p[:, :, 1]
        return jnp.einsum("bhqk,bkhv->bqhv", att.astype(v.dtype), v, preferred_element_type=jnp.float32)

    o = lax.map(one_block, (qb, jnp.arange(nb) * Q_BLOCK))
    return o.transpose(1, 0, 2, 3, 4).reshape(Bsz, T, H, 2 * DIFF_HEAD_DIM)


def diff_attention_layer(x, k_sh, v_sh, w_q, lambda_q, lambda_k, norm_g, w_out, lambda_init):
    Bsz, T, _ = x.shape
    q = (x @ w_q).reshape(Bsz, T, DIFF_HEADS, 2, DIFF_HEAD_DIM) * (DIFF_HEAD_DIM ** -0.5)
    lq = lambda_q.astype(jnp.float32)
    lk = lambda_k.astype(jnp.float32)
    lam = jnp.exp(jnp.sum(lq[0] * lk[0])) - jnp.exp(jnp.sum(lq[1] * lk[1])) + lambda_init
    o = diff_softmax_attention(q, k_sh, v_sh, lam)
    o = rms_norm(o, norm_g.astype(jnp.float32), NORM_EPS) * (1.0 - lambda_init)
    return o.reshape(Bsz, T, DIFF_WIDTH).astype(x.dtype) @ w_out


def setup_inputs(seed: int = 0) -> dict:
    key = jax.random.key(seed)
    ks = jax.random.split(key, 20)
    nrm = jax.random.normal
    f32 = jnp.float32
    x = nrm(ks[0], (BATCH, SEQ, D_MODEL), f32)
    ln_g = 1.0 + 0.02 * nrm(ks[1], (DEPTH, 3, D_MODEL), f32)
    ln_b = 0.02 * nrm(ks[2], (DEPTH, 3, D_MODEL), f32)
    ffn1_w_in = nrm(ks[3], (DEPTH, D_MODEL, 2 * D_FF), f32) * D_MODEL ** -0.5
    ffn1_w_out = nrm(ks[4], (DEPTH, D_FF, D_MODEL), f32) * (D_FF ** -0.5 * BETA_INIT)
    ffn2_w_in = nrm(ks[5], (DEPTH, D_MODEL, 2 * D_FF), f32) * D_MODEL ** -0.5
    ffn2_w_out = nrm(ks[6], (DEPTH, D_FF, D_MODEL), f32) * (D_FF ** -0.5 * BETA_INIT)
    gdn_w_in = nrm(ks[7], (N_A, D_MODEL, GDN_IN), f32) * D_MODEL ** -0.5
    gdn_conv_w = nrm(ks[8], (N_A, GDN_CONV, 3 * GDN_WIDTH), f32) * GDN_CONV ** -0.5
    gdn_a_log = jnp.log(jax.random.uniform(ks[9], (N_A, GDN_HEADS), f32, 1.0, 16.0))
    dt = jnp.exp(jax.random.uniform(ks[10], (N_A, GDN_HEADS), f32) * (math.log(0.1) - math.log(0.001)) + math.log(0.001))
    gdn_dt_bias = dt + jnp.log(-jnp.expm1(-dt))
    gdn_norm_g = 1.0 + 0.02 * nrm(ks[11], (N_A, GDN_HEAD_DIM), f32)
    gdn_w_out = nrm(ks[12], (N_A, GDN_WIDTH, D_MODEL), f32) * (GDN_WIDTH ** -0.5 * BETA_INIT)
    diff_w_kv = nrm(ks[13], (D_MODEL, 2 * DIFF_WIDTH), f32) * D_MODEL ** -0.5
    diff_lambda_k = 0.1 * nrm(ks[14], (2, DIFF_HEAD_DIM), f32)
    diff_w_q = nrm(ks[15], (N_B, D_MODEL, DIFF_WIDTH), f32) * D_MODEL ** -0.5
    diff_lambda_q = 0.1 * nrm(ks[16], (N_B, 2, DIFF_HEAD_DIM), f32)
    diff_norm_g = 1.0 + 0.02 * nrm(ks[17], (N_B, 2 * DIFF_HEAD_DIM), f32)
    diff_w_out = nrm(ks[18], (N_B, DIFF_WIDTH, D_MODEL), f32) * (DIFF_WIDTH ** -0.5 * BETA_INIT)
    return {"x": x, "ln_g": ln_g, "ln_b": ln_b,
            "ffn1_w_in": ffn1_w_in, "ffn1_w_out": ffn1_w_out,
            "ffn2_w_in": ffn2_w_in, "ffn2_w_out": ffn2_w_out,
            "gdn_w_in": gdn_w_in, "gdn_conv_w": gdn_conv_w, "gdn_a_log": gdn_a_log,
            "gdn_dt_bias": gdn_dt_bias, "gdn_norm_g": gdn_norm_g, "gdn_w_out": gdn_w_out,
            "diff_w_kv": diff_w_kv, "diff_lambda_k": diff_lambda_k, "diff_w_q": diff_w_q,
            "diff_lambda_q": diff_lambda_q, "diff_norm_g": diff_norm_g, "diff_w_out": diff_w_out}


def reference(x, ln_g, ln_b, ffn1_w_in, ffn1_w_out, ffn2_w_in, ffn2_w_out,
              gdn_w_in, gdn_conv_w, gdn_a_log, gdn_dt_bias, gdn_norm_g, gdn_w_out,
              diff_w_kv, diff_lambda_k, diff_w_q, diff_lambda_q, diff_norm_g, diff_w_out):
    Bsz, T, _ = x.shape
    k_sh = None
    v_sh = None
    for l in range(DEPTH):
        x = layer_norm(ALPHA * x + 0.5 * swiglu(x, ffn1_w_in[l], ffn1_w_out[l]), ln_g[l, 0], ln_b[l, 0])
        if l < N_A:
            mix = gated_deltanet(x, gdn_w_in[l], gdn_conv_w[l], gdn_a_log[l], gdn_dt_bias[l],
                                 gdn_norm_g[l], gdn_w_out[l])
        else:
            j = l - N_A
            lambda_init = 0.8 - 0.6 * math.exp(-0.3 * l)
            mix = diff_attention_layer(x, k_sh, v_sh, diff_w_q[j], diff_lambda_q[j], diff_lambda_k,
                                       diff_norm_g[j], diff_w_out[j], lambda_init)
        x = layer_norm(ALPHA * x + mix, ln_g[l, 1], ln_b[l, 1])
        x = layer_norm(ALPHA * x + 0.5 * swiglu(x, ffn2_w_in[l], ffn2_w_out[l]), ln_g[l, 2], ln_b[l, 2])
        if l == N_A - 1:
            kv = x @ diff_w_kv
            k_sh = kv[..., :DIFF_WIDTH].reshape(Bsz, T, DIFF_HEADS, 2, DIFF_HEAD_DIM)
            v_sh = kv[..., DIFF_WIDTH:].reshape(Bsz, T, DIFF_HEADS, 2 * DIFF_HEAD_DIM)
    return x
```

```python
import functools
import math

import jax
import jax.numpy as jnp
from jax import lax
from jax.experimental import pallas as pl
from jax.experimental.pallas import tpu as pltpu

F32 = jnp.float32
BF16 = jnp.bfloat16

D_MODEL = 1024
DEPTH = 4
N_A = DEPTH // 2
ALPHA = (2.0 * DEPTH) ** 0.25
LN_EPS = 1e-5
D_FF = 2816
HEAD_W = 128
N_HEADS = D_MODEL // HEAD_W
GDN_CONV = 4
GDN_CHUNK = 64
PAIR = 2 * GDN_CHUNK
DIFF_HEAD_DIM = 64
NORM_EPS = 1e-5
LOG2E = 1.4426950408889634

VMEM_LIMIT_V7X = 56 * 1024 * 1024

FFN_TM = 512
FFN_FC = 256
LIN_TM = 512
GDN_IN_TM = 256
GDN_TM = 512
ATT_TQ = 512
ATT_TK = 512


def _params(sem):
    return pltpu.CompilerParams(dimension_semantics=sem, vmem_limit_bytes=VMEM_LIMIT_V7X)


def _const_spec(shape):
    nd = len(shape)
    return pl.BlockSpec(shape, lambda *_: (0,) * nd, pipeline_mode=pl.Buffered(1))


def _layer_norm(y, g, b):
    mu = jnp.mean(y, axis=-1, keepdims=True)
    yc = y - mu
    var = jnp.mean(yc * yc, axis=-1, keepdims=True)
    return yc * lax.rsqrt(var + LN_EPS) * g + b


def _dot(a, b):
    return jnp.dot(a, b, preferred_element_type=F32)


def _dot_nt(a, b):
    return lax.dot_general(a, b, (((1,), (1,)), ((), ())), preferred_element_type=F32)


def _dot_tn(a, b):
    return lax.dot_general(a, b, (((0,), (0,)), ((), ())), preferred_element_type=F32)


def _silu(x):
    return x * jax.nn.sigmoid(x)


def _ffn_ln_kernel(x_ref, wg_ref, wu_ref, wo_ref, g_ref, b_ref, o_ref, act_ref):
    x = x_ref[...]
    xb = x.astype(BF16)
    for c in range(D_FF // FFN_FC):
        sl = slice(c * FFN_FC, (c + 1) * FFN_FC)
        gate = _dot(xb, wg_ref[:, sl])
        up = _dot(xb, wu_ref[:, sl])
        act_ref[:, sl] = (_silu(gate) * up).astype(BF16)
    h = _dot(act_ref[...], wo_ref[...])
    o_ref[...] = _layer_norm(ALPHA * x + 0.5 * h, g_ref[...], b_ref[...])


def _ffn_ln(x, wg, wu, wo, g, b):
    n = x.shape[0]
    return pl.pallas_call(
        _ffn_ln_kernel,
        grid=(n // FFN_TM,),
        in_specs=[
            pl.BlockSpec((FFN_TM, D_MODEL), lambda i: (i, 0)),
            _const_spec((D_MODEL, D_FF)),
            _const_spec((D_MODEL, D_FF)),
            _const_spec((D_FF, D_MODEL)),
            _const_spec((1, D_MODEL)),
            _const_spec((1, D_MODEL)),
        ],
        out_specs=pl.BlockSpec((FFN_TM, D_MODEL), lambda i: (i, 0)),
        out_shape=jax.ShapeDtypeStruct((n, D_MODEL), F32),
        scratch_shapes=[pltpu.VMEM((FFN_TM, D_FF), BF16)],
        compiler_params=_params(("arbitrary",)),
        name="ffn_ln",
    )(x, wg, wu, wo, g, b)


def _proj_kernel(x_ref, w_ref, *o_refs, scale):
    xb = x_ref[...].astype(BF16)
    for j, o_ref in enumerate(o_refs):
        y = _dot(xb, w_ref[:, j * D_MODEL:(j + 1) * D_MODEL])
        if scale != 1.0:
            y = y * scale
        o_ref[...] = y.astype(o_ref.dtype)


def _proj(x, w, scale=1.0):
    n = x.shape[0]
    n_out = w.shape[1] // D_MODEL
    outs = pl.pallas_call(
        functools.partial(_proj_kernel, scale=scale),
        grid=(n // LIN_TM,),
        in_specs=[
            pl.BlockSpec((LIN_TM, D_MODEL), lambda i: (i, 0)),
            _const_spec(w.shape),
        ],
        out_specs=[pl.BlockSpec((LIN_TM, D_MODEL), lambda i: (i, 0))] * n_out,
        out_shape=[jax.ShapeDtypeStruct((n, D_MODEL), BF16)] * n_out,
        compiler_params=_params(("arbitrary",)),
        name="proj",
    )(x, w)
    return outs


def _out_ln_kernel(a_ref, x_ref, w_ref, g_ref, b_ref, o_ref):
    mix = _dot(a_ref[...], w_ref[...])
    o_ref[...] = _layer_norm(ALPHA * x_ref[...] + mix, g_ref[...], b_ref[...])


def _out_ln(a, x, w, g, b):
    n = x.shape[0]
    return pl.pallas_call(
        _out_ln_kernel,
        grid=(n // LIN_TM,),
        in_specs=[
            pl.BlockSpec((LIN_TM, D_MODEL), lambda i: (i, 0)),
            pl.BlockSpec((LIN_TM, D_MODEL), lambda i: (i, 0)),
            _const_spec((D_MODEL, D_MODEL)),
            _const_spec((1, D_MODEL)),
            _const_spec((1, D_MODEL)),
        ],
        out_specs=pl.BlockSpec((LIN_TM, D_MODEL), lambda i: (i, 0)),
        out_shape=jax.ShapeDtypeStruct((n, D_MODEL), F32),
        compiler_params=_params(("arbitrary",)),
        name="out_ln",
    )(a, x, w, g, b)


def _split3(x):
    x1 = x.astype(BF16)
    r1 = x - x1.astype(F32)
    x2 = r1.astype(BF16)
    x3 = (r1 - x2.astype(F32)).astype(BF16)
    return x1, x2, x3


def _gdn_in_kernel(x_ref, wqkv_ref, wz_ref, wb_ref, wa_ref, cw_ref, alog_ref, dtb_ref,
                   q_ref, k_ref, v_ref, gate_ref, beta_ref, gc_ref, ext_ref):
    tm = GDN_IN_TM
    w3 = 3 * D_MODEL

    @pl.when(pl.program_id(1) == 0)
    def _():
        ext_ref[0:8, :] = jnp.zeros((8, w3), F32)

    xb = x_ref[0].astype(BF16)
    ext_ref[8:8 + tm, :] = _dot(xb, wqkv_ref[...])

    outs = (q_ref, k_ref, v_ref)
    for cb in range(3 * N_HEADS):
        sl = slice(cb * HEAD_W, (cb + 1) * HEAD_W)
        y = cw_ref[3:4, sl] * ext_ref[8:8 + tm, sl]
        for j in range(GDN_CONV - 1):
            y = y + cw_ref[j:j + 1, sl] * ext_ref[5 + j:5 + j + tm, sl]
        y = _silu(y)
        which, h = divmod(cb, N_HEADS)
        if which < 2:
            ss = jnp.sum(y * y, axis=-1, keepdims=True)
            y = y * lax.rsqrt(ss + 1e-6)
            if which == 0:
                y = y * (HEAD_W ** -0.5)
        outs[which][0, :, h * HEAD_W:(h + 1) * HEAD_W] = y

    ext_ref[0:8, :] = ext_ref[tm:tm + 8, :]

    gate_ref[0] = _silu(_dot(xb, wz_ref[...]))
    beta_ref[0] = jax.nn.sigmoid(_dot(xb, wb_ref[...]))
    a = _dot(xb, wa_ref[...]) + dtb_ref[...]
    softplus = jnp.maximum(a, 0.0) + jnp.log(1.0 + jnp.exp(-jnp.abs(a)))
    g = -jnp.exp(alog_ref[...]) * softplus
    row = lax.broadcasted_iota(jnp.int32, (tm, tm), 0)
    col = lax.broadcasted_iota(jnp.int32, (tm, tm), 1)
    tri = ((row // GDN_CHUNK == col // GDN_CHUNK) & (col <= row)).astype(BF16)
    g1, g2, g3 = _split3(g)
    gc_ref[0] = _dot(tri, g1) + _dot(tri, g2) + _dot(tri, g3)


def _gdn_in(x3, wqkv, wz, wb, wa, cw, alog, dtb):
    bsz, t, _ = x3.shape
    tm = GDN_IN_TM
    blk = pl.BlockSpec((1, tm, D_MODEL), lambda b, i: (b, i, 0))
    return pl.pallas_call(
        _gdn_in_kernel,
        grid=(bsz, t // tm),
        in_specs=[
            blk,
            _const_spec(wqkv.shape), _const_spec(wz.shape), _const_spec(wb.shape), _const_spec(wa.shape),
            _const_spec(cw.shape), _const_spec(alog.shape), _const_spec(dtb.shape),
        ],
        out_specs=[blk] * 6,
        out_shape=[jax.ShapeDtypeStruct((bsz, t, D_MODEL), F32)] * 6,
        scratch_shapes=[pltpu.VMEM((tm + 8, 3 * D_MODEL), F32)],
        compiler_params=_params(("arbitrary", "arbitrary")),
        name="gdn_in",
    )(x3, wqkv, wz, wb, wa, cw, alog, dtb)


def _mm(a, b):
    return _dot(a.astype(BF16), b.astype(BF16))


def _mm3(a, b):
    a1 = a.astype(BF16)
    a2 = (a - a1.astype(F32)).astype(BF16)
    b1 = b.astype(BF16)
    b2 = (b - b1.astype(F32)).astype(BF16)
    return _dot(a1, b1) + _dot(a2, b1) + _dot(a1, b2)


def _gdn_chunk_kernel(q_ref, k_ref, v_ref, gate_ref, beta_ref, gc_ref, ng_ref, o_ref, s_ref):
    @pl.when(pl.program_id(1) == 0)
    def _():
        s_ref[...] = jnp.zeros_like(s_ref)

    row = lax.broadcasted_iota(jnp.int32, (PAIR, PAIR), 0)
    col = lax.broadcasted_iota(jnp.int32, (PAIR, PAIR), 1)
    same64 = (row // 64) == (col // 64)
    causal = same64 & (col <= row)
    strict = same64 & (col < row)
    same32 = (row // 32) == (col // 32)
    same16 = (row // 16) == (col // 16)
    m16 = (strict & same16).astype(F32)
    m32 = (strict & same32 & ~same16).astype(F32)
    m64 = (strict & ~same32).astype(F32)
    eye = (row == col).astype(F32)
    first = lax.broadcasted_iota(jnp.int32, (PAIR, HEAD_W), 0) < GDN_CHUNK
    ng = ng_ref[...]

    def pair_body(p, carry):
        r0 = pl.multiple_of(p * PAIR, PAIR)
        rows = pl.ds(r0, PAIR)
        for h in range(N_HEADS):
            cols = slice(h * HEAD_W, (h + 1) * HEAD_W)
            q = q_ref[0, rows, cols]
            k = k_ref[0, rows, cols]
            v = v_ref[0, rows, cols]
            beta = beta_ref[0, rows, cols]
            gc = gc_ref[0, rows, cols]

            gct = gc.T
            d = jnp.where(causal, gc - gct, 0.0)
            decay = jnp.where(causal, jnp.exp(d), 0.0)
            kb = k * beta
            kbf = k.astype(BF16)
            a_mat = _dot_nt(kb.astype(BF16), kbf) * decay
            attn = _dot_nt(q.astype(BF16), kbf) * decay
            egc = jnp.exp(gc)

            nneg = -(a_mat * m16)
            pinv = eye + nneg
            n2 = _mm3(nneg, nneg)
            pinv = pinv + _mm3(pinv, n2)
            n4 = _mm3(n2, n2)
            pinv = pinv + _mm3(pinv, n4)
            n8 = _mm3(n4, n4)
            pinv = pinv + _mm3(pinv, n8)
            pinv = pinv - _mm3(pinv, _mm3(a_mat * m32, pinv))
            tinv = pinv - _mm3(pinv, _mm3(a_mat * m64, pinv))

            rhs = jnp.concatenate([kb * egc, v * beta], axis=1)
            wu = _mm(tinv, rhs)
            w = wu[:, :HEAD_W]
            u = wu[:, HEAD_W:]

            gl = jnp.where(first, gc[GDN_CHUNK - 1:GDN_CHUNK, :], gc[PAIR - 1:PAIR, :])
            kd = k * jnp.exp(gl - gc)
            qd = q * egc
            cd = jnp.exp(gl)

            s = s_ref[h]
            wb16 = w.astype(BF16)
            qb16 = qd.astype(BF16)
            vns = []
            qss = []
            for c in range(2):
                rs = slice(c * GDN_CHUNK, (c + 1) * GDN_CHUNK)
                sb = s.astype(BF16)
                vn = u[rs] - _dot(wb16[rs], sb)
                qss.append(_dot(qb16[rs], sb))
                s = s * cd[c * GDN_CHUNK:c * GDN_CHUNK + 1, :] + _dot_tn(kd[rs].astype(BF16), vn.astype(BF16))
                vns.append(vn)
            s_ref[h] = s
            vn = jnp.concatenate(vns, axis=0)
            o = jnp.concatenate(qss, axis=0) + _mm(attn, vn)

            ms = jnp.mean(o * o, axis=-1, keepdims=True)
            o = o * lax.rsqrt(ms + NORM_EPS) * ng * gate_ref[0, rows, cols]
            o_ref[0, rows, cols] = o.astype(o_ref.dtype)
        return carry

    lax.fori_loop(0, GDN_TM // PAIR, pair_body, 0)


def _gdn_chunk(q, k, v, gate, beta, gc, ng):
    bsz, t, _ = q.shape
    tm = GDN_TM
    blk = pl.BlockSpec((1, tm, D_MODEL), lambda b, i: (b, i, 0))
    return pl.pallas_call(
        _gdn_chunk_kernel,
        grid=(bsz, t // tm),
        in_specs=[blk] * 6 + [_const_spec(ng.shape)],
        out_specs=blk,
        out_shape=jax.ShapeDtypeStruct((bsz, t, D_MODEL), BF16),
        scratch_shapes=[pltpu.VMEM((N_HEADS, HEAD_W, HEAD_W), F32)],
        compiler_params=_params(("arbitrary", "arbitrary")),
        name="gdn_chunk",
    )(q, k, v, gate, beta, gc, ng)


def _diff_attn_kernel(q_ref, k_ref, v_ref, lq_ref, lk_ref, ng_ref, o_ref, *, lambda_init):
    tq, tk = ATT_TQ, ATT_TK
    i = pl.program_id(2)
    q = q_ref[0]
    lane = lax.broadcasted_iota(jnp.int32, (tq, HEAD_W), 1)
    zero = jnp.zeros_like(q)
    qs = (jnp.where(lane < DIFF_HEAD_DIM, q, zero), jnp.where(lane >= DIFF_HEAD_DIM, q, zero))

    def tile(j, carry, masked):
        k = k_ref[0, pl.ds(pl.multiple_of(j * tk, tk), tk), :]
        v = v_ref[0, pl.ds(pl.multiple_of(j * tk, tk), tk), :]
        out = []
        for c in range(2):
            m, l, acc = carry[c]
            s = _dot_nt(qs[c], k)
            if masked:
                r = lax.broadcasted_iota(jnp.int32, (tq, tk), 0)
                cc = lax.broadcasted_iota(jnp.int32, (tq, tk), 1)
                s = jnp.where(cc <= r, s, -jnp.inf)
            m_new = jnp.maximum(m, jnp.max(s, axis=-1, keepdims=True))
            alpha = jnp.exp2(m - m_new)
            p = jnp.exp2(s - m_new)
            l = alpha * l + jnp.sum(p, axis=-1, keepdims=True)
            acc = alpha * acc + _dot(p.astype(BF16), v)
            out.append((m_new, l, acc))
        return tuple(out)

    init = tuple((jnp.full((tq, 1), -jnp.inf, F32), jnp.zeros((tq, 1), F32), jnp.zeros((tq, HEAD_W), F32))
                 for _ in range(2))
    carry = lax.fori_loop(0, i, lambda j, c: tile(j, c, False), init)
    (_, l1, acc1), (_, l2, acc2) = tile(i, carry, True)

    lqk = lq_ref[...] * lk_ref[...]
    lam = (jnp.exp(jnp.sum(lqk[0:1, :], axis=-1, keepdims=True))
           - jnp.exp(jnp.sum(lqk[1:2, :], axis=-1, keepdims=True)) + lambda_init)
    o = acc1 / l1 - lam * (acc2 / l2)
    ms = jnp.mean(o * o, axis=-1, keepdims=True)
    o = o * lax.rsqrt(ms + NORM_EPS) * ng_ref[...] * (1.0 - lambda_init)
    o_ref[0] = o.astype(o_ref.dtype)


def _diff_attn(q, k, v, lq, lk, ng, lambda_init):
    bsz, t, _ = q.shape
    tq = ATT_TQ
    return pl.pallas_call(
        functools.partial(_diff_attn_kernel, lambda_init=lambda_init),
        grid=(bsz, N_HEADS, t // tq),
        in_specs=[
            pl.BlockSpec((1, tq, HEAD_W), lambda b, h, i: (b, i, h)),
            pl.BlockSpec((1, t, HEAD_W), lambda b, h, i: (b, 0, h)),
            pl.BlockSpec((1, t, HEAD_W), lambda b, h, i: (b, 0, h)),
            _const_spec(lq.shape), _const_spec(lk.shape), _const_spec(ng.shape),
        ],
        out_specs=pl.BlockSpec((1, tq, HEAD_W), lambda b, h, i: (b, i, h)),
        out_shape=jax.ShapeDtypeStruct((bsz, t, D_MODEL), BF16),
        compiler_params=_params(("arbitrary", "arbitrary", "arbitrary")),
        name="diff_attn",
    )(q, k, v, lq, lk, ng)


def kernel(x, ln_g, ln_b, ffn1_w_in, ffn1_w_out, ffn2_w_in, ffn2_w_out, gdn_w_in, gdn_conv_w, gdn_a_log,
           gdn_dt_bias, gdn_norm_g, gdn_w_out, diff_w_kv, diff_lambda_k, diff_w_q, diff_lambda_q, diff_norm_g,
           diff_w_out):
    bsz, t, d = x.shape
    n = bsz * t
    xs = x.reshape(n, d)
    w = 4 * D_MODEL

    def ffn(xs, w_in, w_out, g, b):
        return _ffn_ln(xs, w_in[:, :D_FF].astype(BF16), w_in[:, D_FF:].astype(BF16), w_out.astype(BF16),
                       g.reshape(1, d), b.reshape(1, d))

    k_sh = v_sh = None
    for l in range(DEPTH):
        xs = ffn(xs, ffn1_w_in[l], ffn1_w_out[l], ln_g[l, 0], ln_b[l, 0])
        if l < N_A:
            w_in = gdn_w_in[l]
            rep = lambda a: jnp.repeat(a, HEAD_W, axis=-1)
            q, k, v, gate, beta, gc = _gdn_in(
                xs.reshape(bsz, t, d),
                w_in[:, :3 * D_MODEL].astype(BF16), w_in[:, 3 * D_MODEL:w].astype(BF16),
                rep(w_in[:, w:w + N_HEADS]).astype(BF16), rep(w_in[:, w + N_HEADS:]).astype(BF16),
                gdn_conv_w[l], rep(gdn_a_log[l]).reshape(1, d), rep(gdn_dt_bias[l]).reshape(1, d))
            o = _gdn_chunk(q, k, v, gate, beta, gc, gdn_norm_g[l].reshape(1, HEAD_W))
            w_out = gdn_w_out[l]
        else:
            j = l - N_A
            lambda_init = 0.8 - 0.6 * math.exp(-0.3 * l)
            (q,) = _proj(xs, diff_w_q[j].astype(BF16), scale=DIFF_HEAD_DIM ** -0.5 * LOG2E)
            o = _diff_attn(q.reshape(bsz, t, d), k_sh, v_sh, diff_lambda_q[j], diff_lambda_k,
                           diff_norm_g[j].reshape(1, HEAD_W), lambda_init)
            w_out = diff_w_out[j]
        xs = _out_ln(o.reshape(n, d), xs, w_out.astype(BF16), ln_g[l, 1].reshape(1, d), ln_b[l, 1].reshape(1, d))
        xs = ffn(xs, ffn2_w_in[l], ffn2_w_out[l], ln_g[l, 2], ln_b[l, 2])
        if l == N_A - 1:
            k_sh, v_sh = _proj(xs, diff_w_kv.astype(BF16))
            k_sh = k_sh.reshape(bsz, t, d)
            v_sh = v_sh.reshape(bsz, t, d)
    return xs.reshape(bsz, t, d)
```

```python
import functools
import math

import jax
import jax.numpy as jnp
from jax import lax
from jax.experimental import pallas as pl
from jax.experimental.pallas import tpu as pltpu

F32 = jnp.float32
BF16 = jnp.bfloat16

D_MODEL = 1024
DEPTH = 4
N_A = DEPTH // 2
ALPHA = (2.0 * DEPTH) ** 0.25
LN_EPS = 1e-5
D_FF = 2816
HEAD_W = 128
N_HEADS = D_MODEL // HEAD_W
GDN_CONV = 4
GDN_CHUNK = 64
PAIR = 2 * GDN_CHUNK
DIFF_HEAD_DIM = 64
NORM_EPS = 1e-5
LOG2E = 1.4426950408889634

VMEM_LIMIT_V7X = 56 * 1024 * 1024

FFN_TM = 512
FFN_FC = 256
LIN_TM = 512
GDN_IN_TM = 256
GDN_TM = 512
ATT_TQ = 512
ATT_TK = ATT_TQ // 2
MASK_VALUE = -1e30


def _params(sem):
    return pltpu.CompilerParams(dimension_semantics=sem, vmem_limit_bytes=VMEM_LIMIT_V7X)


def _const_spec(shape):
    nd = len(shape)
    return pl.BlockSpec(shape, lambda *_: (0,) * nd, pipeline_mode=pl.Buffered(1))


def _layer_norm(y, g, b):
    mu = jnp.mean(y, axis=-1, keepdims=True)
    yc = y - mu
    var = jnp.mean(yc * yc, axis=-1, keepdims=True)
    return yc * lax.rsqrt(var + LN_EPS) * g + b


def _dot(a, b):
    return jnp.dot(a, b, preferred_element_type=F32)


def _dot_nt(a, b):
    return lax.dot_general(a, b, (((1,), (1,)), ((), ())), preferred_element_type=F32)


def _dot_tn(a, b):
    return lax.dot_general(a, b, (((0,), (0,)), ((), ())), preferred_element_type=F32)


def _silu(x):
    return x * jax.nn.sigmoid(x)


def _ffn_ln_kernel(x_ref, wg_ref, wu_ref, wo_ref, g_ref, b_ref, o_ref, act_ref):
    x = x_ref[...]
    xb = x.astype(BF16)
    for c in range(D_FF // FFN_FC):
        sl = slice(c * FFN_FC, (c + 1) * FFN_FC)
        gate = _dot(xb, wg_ref[:, sl])
        up = _dot(xb, wu_ref[:, sl])
        act_ref[:, sl] = (_silu(gate) * up).astype(BF16)
    h = _dot(act_ref[...], wo_ref[...])
    o_ref[...] = _layer_norm(ALPHA * x + 0.5 * h, g_ref[...], b_ref[...])


def _ffn_ln(x, wg, wu, wo, g, b):
    n = x.shape[0]
    return pl.pallas_call(
        _ffn_ln_kernel,
        grid=(n // FFN_TM,),
        in_specs=[
            pl.BlockSpec((FFN_TM, D_MODEL), lambda i: (i, 0)),
            _const_spec((D_MODEL, D_FF)),
            _const_spec((D_MODEL, D_FF)),
            _const_spec((D_FF, D_MODEL)),
            _const_spec((1, D_MODEL)),
            _const_spec((1, D_MODEL)),
        ],
        out_specs=pl.BlockSpec((FFN_TM, D_MODEL), lambda i: (i, 0)),
        out_shape=jax.ShapeDtypeStruct((n, D_MODEL), F32),
        scratch_shapes=[pltpu.VMEM((FFN_TM, D_FF), BF16)],
        compiler_params=_params(("arbitrary",)),
        name="ffn_ln",
    )(x, wg, wu, wo, g, b)


def _proj_kernel(x_ref, w_ref, *o_refs, scale):
    xb = x_ref[...].astype(BF16)
    for j, o_ref in enumerate(o_refs):
        y = _dot(xb, w_ref[:, j * D_MODEL:(j + 1) * D_MODEL])
        if scale != 1.0:
            y = y * scale
        o_ref[...] = y.astype(o_ref.dtype)


def _proj(x, w, scale=1.0):
    n = x.shape[0]
    n_out = w.shape[1] // D_MODEL
    outs = pl.pallas_call(
        functools.partial(_proj_kernel, scale=scale),
        grid=(n // LIN_TM,),
        in_specs=[
            pl.BlockSpec((LIN_TM, D_MODEL), lambda i: (i, 0)),
            _const_spec(w.shape),
        ],
        out_specs=[pl.BlockSpec((LIN_TM, D_MODEL), lambda i: (i, 0))] * n_out,
        out_shape=[jax.ShapeDtypeStruct((n, D_MODEL), BF16)] * n_out,
        compiler_params=_params(("arbitrary",)),
        name="proj",
    )(x, w)
    return outs


def _out_ln_kernel(a_ref, x_ref, w_ref, g_ref, b_ref, o_ref):
    mix = _dot(a_ref[...], w_ref[...])
    o_ref[...] = _layer_norm(ALPHA * x_ref[...] + mix, g_ref[...], b_ref[...])


def _out_ln(a, x, w, g, b):
    n = x.shape[0]
    return pl.pallas_call(
        _out_ln_kernel,
        grid=(n // LIN_TM,),
        in_specs=[
            pl.BlockSpec((LIN_TM, D_MODEL), lambda i: (i, 0)),
            pl.BlockSpec((LIN_TM, D_MODEL), lambda i: (i, 0)),
            _const_spec((D_MODEL, D_MODEL)),
            _const_spec((1, D_MODEL)),
            _const_spec((1, D_MODEL)),
        ],
        out_specs=pl.BlockSpec((LIN_TM, D_MODEL), lambda i: (i, 0)),
        out_shape=jax.ShapeDtypeStruct((n, D_MODEL), F32),
        compiler_params=_params(("arbitrary",)),
        name="out_ln",
    )(a, x, w, g, b)


def _split3(x):
    x1 = x.astype(BF16)
    r1 = x - x1.astype(F32)
    x2 = r1.astype(BF16)
    x3 = (r1 - x2.astype(F32)).astype(BF16)
    return x1, x2, x3


def _gdn_in_kernel(x_ref, wqkv_ref, wz_ref, wb_ref, wa_ref, cw_ref, alog_ref, dtb_ref,
                   q_ref, k_ref, v_ref, gate_ref, beta_ref, gc_ref, ext_ref):
    tm = GDN_IN_TM
    w3 = 3 * D_MODEL

    @pl.when(pl.program_id(1) == 0)
    def _():
        ext_ref[0:8, :] = jnp.zeros((8, w3), F32)

    xb = x_ref[0].astype(BF16)
    ext_ref[8:8 + tm, :] = _dot(xb, wqkv_ref[...])

    outs = (q_ref, k_ref, v_ref)
    for cb in range(3 * N_HEADS):
        sl = slice(cb * HEAD_W, (cb + 1) * HEAD_W)
        y = cw_ref[3:4, sl] * ext_ref[8:8 + tm, sl]
        for j in range(GDN_CONV - 1):
            y = y + cw_ref[j:j + 1, sl] * ext_ref[5 + j:5 + j + tm, sl]
        y = _silu(y)
        which, h = divmod(cb, N_HEADS)
        if which < 2:
            ss = jnp.sum(y * y, axis=-1, keepdims=True)
            y = y * lax.rsqrt(ss + 1e-6)
            if which == 0:
                y = y * (HEAD_W ** -0.5)
        outs[which][0, :, h * HEAD_W:(h + 1) * HEAD_W] = y

    ext_ref[0:8, :] = ext_ref[tm:tm + 8, :]

    gate_ref[0] = _silu(_dot(xb, wz_ref[...]))
    beta_ref[0] = jax.nn.sigmoid(_dot(xb, wb_ref[...]))
    a = _dot(xb, wa_ref[...]) + dtb_ref[...]
    softplus = jnp.maximum(a, 0.0) + jnp.log(1.0 + jnp.exp(-jnp.abs(a)))
    g = -jnp.exp(alog_ref[...]) * softplus
    row = lax.broadcasted_iota(jnp.int32, (tm, tm), 0)
    col = lax.broadcasted_iota(jnp.int32, (tm, tm), 1)
    tri = ((row // GDN_CHUNK == col // GDN_CHUNK) & (col <= row)).astype(BF16)
    g1, g2, g3 = _split3(g)
    gc_ref[0] = _dot(tri, g1) + _dot(tri, g2) + _dot(tri, g3)


def _gdn_in(x3, wqkv, wz, wb, wa, cw, alog, dtb):
    bsz, t, _ = x3.shape
    tm = GDN_IN_TM
    blk = pl.BlockSpec((1, tm, D_MODEL), lambda b, i: (b, i, 0))
    return pl.pallas_call(
        _gdn_in_kernel,
        grid=(bsz, t // tm),
        in_specs=[
            blk,
            _const_spec(wqkv.shape), _const_spec(wz.shape), _const_spec(wb.shape), _const_spec(wa.shape),
            _const_spec(cw.shape), _const_spec(alog.shape), _const_spec(dtb.shape),
        ],
        out_specs=[blk] * 6,
        out_shape=[jax.ShapeDtypeStruct((bsz, t, D_MODEL), F32)] * 6,
        scratch_shapes=[pltpu.VMEM((tm + 8, 3 * D_MODEL), F32)],
        compiler_params=_params(("arbitrary", "arbitrary")),
        name="gdn_in",
    )(x3, wqkv, wz, wb, wa, cw, alog, dtb)


def _mm(a, b):
    return _dot(a.astype(BF16), b.astype(BF16))


def _gdn_chunk_kernel(q_ref, k_ref, v_ref, gate_ref, beta_ref, gc_ref, ng_ref, o_ref, s_ref):
    @pl.when(pl.program_id(1) == 0)
    def _():
        s_ref[...] = jnp.zeros_like(s_ref)

    row = lax.broadcasted_iota(jnp.int32, (PAIR, PAIR), 0)
    col = lax.broadcasted_iota(jnp.int32, (PAIR, PAIR), 1)
    same64 = (row // 64) == (col // 64)
    causal = same64 & (col <= row)
    strict = same64 & (col < row)
    same32 = (row // 32) == (col // 32)
    same16 = (row // 16) == (col // 16)
    m16 = (strict & same16).astype(F32)
    m32 = (strict & same32 & ~same16).astype(F32)
    m64 = (strict & ~same32).astype(F32)
    first = lax.broadcasted_iota(jnp.int32, (PAIR, HEAD_W), 0) < GDN_CHUNK
    ng = ng_ref[...]
    heads = range(N_HEADS)

    def each(fn, *lists):
        return [fn(*args) for args in zip(*lists)]

    def pair_body(p, carry):
        rows = pl.ds(pl.multiple_of(p * PAIR, PAIR), PAIR)

        def load(ref):
            return [ref[0, rows, h * HEAD_W:(h + 1) * HEAD_W] for h in heads]

        q, k, v, beta, gc = load(q_ref), load(k_ref), load(v_ref), load(beta_ref), load(gc_ref)

        def decay_of(g):
            d = jnp.where(causal, g - g.T, 0.0)
            return jnp.where(causal, jnp.exp(d), 0.0)

        decay = each(decay_of, gc)
        kb = each(lambda a, b: a * b, k, beta)
        kbf = each(lambda a: a.astype(BF16), k)
        a_mat = each(lambda a, b, c: _dot_nt(a.astype(BF16), b) * c, kb, kbf, decay)
        attn = each(lambda a, b, c: _dot_nt(a.astype(BF16), b) * c, q, kbf, decay)
        egc = each(jnp.exp, gc)

        n1 = each(lambda a: -(a * m16), a_mat)
        n2 = each(_mm, n1, n1)
        x = each(lambda a, b, c: a + b + c, n1, n2, each(_mm, n1, n2))
        n4 = each(_mm, n2, n2)
        x = each(lambda a, b, c: a + b + c, x, n4, each(_mm, x, n4))
        n8 = each(_mm, n4, n4)
        x = each(lambda a, b, c: a + b + c, x, n8, each(_mm, x, n8))
        for mask in (m32, m64):
            low = each(lambda a: a * mask, a_mat)
            t = each(lambda a, b: a + b, low, each(_mm, low, x))
            x = each(lambda a, b, c: a - b - c, x, t, each(_mm, x, t))

        rhs = each(lambda a, b, c, d: jnp.concatenate([a * b, c * d], axis=1), kb, egc, v, beta)
        wu = each(lambda a, b: b + _mm(a, b), x, rhs)
        w = each(lambda a: a[:, :HEAD_W].astype(BF16), wu)
        u = each(lambda a: a[:, HEAD_W:], wu)

        gl = each(lambda g: jnp.where(first, g[GDN_CHUNK - 1:GDN_CHUNK, :], g[PAIR - 1:PAIR, :]), gc)
        kd = each(lambda a, b, c: (a * jnp.exp(b - c)).astype(BF16), k, gl, gc)
        qd = each(lambda a, b: (a * b).astype(BF16), q, egc)
        cd = each(jnp.exp, gl)

        s = [s_ref[h] for h in heads]
        vns, qss = [], []
        for c in range(2):
            rs = slice(c * GDN_CHUNK, (c + 1) * GDN_CHUNK)
            cd_row = slice(c * GDN_CHUNK, c * GDN_CHUNK + 1)
            sb = each(lambda a: a.astype(BF16), s)
            vn = each(lambda a, b, c2: a[rs] - _dot(b[rs], c2), u, w, sb)
            qss.append(each(lambda a, b: _dot(a[rs], b), qd, sb))
            s = each(lambda a, b, c2, d: a * b[cd_row, :] + _dot_tn(c2[rs], d.astype(BF16)), s, cd, kd, vn)
            vns.append(vn)
        for h in heads:
            s_ref[h] = s[h]
        vn = each(lambda a, b: jnp.concatenate([a, b], axis=0), vns[0], vns[1])
        qs = each(lambda a, b: jnp.concatenate([a, b], axis=0), qss[0], qss[1])
        o = each(lambda a, b, c: a + _mm(b, c), qs, attn, vn)
        gate = load(gate_ref)
        for h in heads:
            ms = jnp.mean(o[h] * o[h], axis=-1, keepdims=True)
            oh = o[h] * lax.rsqrt(ms + NORM_EPS) * ng * gate[h]
            o_ref[0, rows, h * HEAD_W:(h + 1) * HEAD_W] = oh.astype(o_ref.dtype)
        return carry

    lax.fori_loop(0, GDN_TM // PAIR, pair_body, 0)


def _gdn_chunk(q, k, v, gate, beta, gc, ng):
    bsz, t, _ = q.shape
    tm = GDN_TM
    blk = pl.BlockSpec((1, tm, D_MODEL), lambda b, i: (b, i, 0))
    return pl.pallas_call(
        _gdn_chunk_kernel,
        grid=(bsz, t // tm),
        in_specs=[blk] * 6 + [_const_spec(ng.shape)],
        out_specs=blk,
        out_shape=jax.ShapeDtypeStruct((bsz, t, D_MODEL), BF16),
        scratch_shapes=[pltpu.VMEM((N_HEADS, HEAD_W, HEAD_W), F32)],
        compiler_params=_params(("arbitrary", "arbitrary")),
        name="gdn_chunk",
    )(q, k, v, gate, beta, gc, ng)


VT_ROWS = HEAD_W + 16


def _kv_proj_kernel(x_ref, wk_ref, wvt_ref, k_ref, vt_ref):
    xb = x_ref[0].astype(BF16)
    k_ref[0] = _dot(xb, wk_ref[...]).astype(BF16)
    vt = _dot_nt(wvt_ref[...], xb).astype(BF16)
    tm = vt.shape[1]
    for h in range(N_HEADS):
        vt_ref[0, h, 0:HEAD_W, :] = vt[h * HEAD_W:(h + 1) * HEAD_W, :]
        vt_ref[0, h, HEAD_W:VT_ROWS, :] = jnp.ones((VT_ROWS - HEAD_W, tm), BF16)


def _kv_proj(x3, wk, wvt):
    bsz, t, _ = x3.shape
    tm = LIN_TM
    return pl.pallas_call(
        _kv_proj_kernel,
        grid=(bsz, t // tm),
        in_specs=[
            pl.BlockSpec((1, tm, D_MODEL), lambda b, i: (b, i, 0)),
            _const_spec(wk.shape), _const_spec(wvt.shape),
        ],
        out_specs=[
            pl.BlockSpec((1, tm, D_MODEL), lambda b, i: (b, i, 0)),
            pl.BlockSpec((1, N_HEADS, VT_ROWS, tm), lambda b, i: (b, 0, 0, i)),
        ],
        out_shape=[
            jax.ShapeDtypeStruct((bsz, t, D_MODEL), BF16),
            jax.ShapeDtypeStruct((bsz, N_HEADS, VT_ROWS, t), BF16),
        ],
        compiler_params=_params(("arbitrary", "arbitrary")),
        name="kv_proj",
    )(x3, wk, wvt)


def _diff_attn_kernel(q_ref, k_ref, vt_ref, lq_ref, lk_ref, ng_ref, o_ref, m_ref, acc_ref, sa_ref, sb_ref, *,
                      lambda_init):
    tq, tk = ATT_TQ, ATT_TK
    i = pl.program_id(2)
    q = q_ref[0]
    lane = lax.broadcasted_iota(jnp.int32, (tq, HEAD_W), 1)
    zero = jnp.zeros_like(q)
    qs = (jnp.where(lane < DIFF_HEAD_DIM, q, zero), jnp.where(lane >= DIFF_HEAD_DIM, q, zero))

    m_ref[...] = jnp.full(m_ref.shape, MASK_VALUE, F32)
    acc_ref[...] = jnp.zeros(acc_ref.shape, F32)

    def key_offset(n):
        t = jnp.where(n < 2, 2 * i + n, n - 2)
        return pl.multiple_of(t * tk, tk)

    def scores(n, dst_ref):
        k = k_ref[0, pl.ds(key_offset(n), tk), :]
        for c in range(2):
            dst_ref[c] = _dot_nt(k, qs[c])

    def consume(n, src_ref, masked):
        off = key_offset(n)
        vt = vt_ref[0, 0, :, pl.ds(off, tk)]
        for c in range(2):
            st = src_ref[c]
            if masked:
                kpos = off + lax.broadcasted_iota(jnp.int32, (tk, tq), 0)
                qpos = i * tq + lax.broadcasted_iota(jnp.int32, (tk, tq), 1)
                st = jnp.where(kpos <= qpos, st, MASK_VALUE)
            m_old = m_ref[c]
            m_new = jnp.maximum(m_old, jnp.max(st, axis=0, keepdims=True))
            alpha = jnp.exp2(m_old - m_new)
            p = jnp.exp2(st - m_new).astype(BF16)
            acc_ref[c] = alpha * acc_ref[c] + _dot(vt, p)
            m_ref[c] = m_new

    scores(0, sa_ref)
    scores(1, sb_ref)
    consume(0, sa_ref, True)
    scores(2, sa_ref)
    consume(1, sb_ref, True)

    def body(pp, carry):
        n = 2 * pp + 2
        scores(n + 1, sb_ref)
        consume(n, sa_ref, False)
        scores(n + 2, sa_ref)
        consume(n + 1, sb_ref, False)
        return carry

    lax.fori_loop(0, i, body, 0)

    lqk = lq_ref[...] * lk_ref[...]
    lam = (jnp.exp(jnp.sum(lqk[0:1, :], axis=-1, keepdims=True))
           - jnp.exp(jnp.sum(lqk[1:2, :], axis=-1, keepdims=True)) + lambda_init)
    a1 = acc_ref[0]
    a2 = acc_ref[1]
    ot = a1[:HEAD_W] / a1[HEAD_W:HEAD_W + 1] - lam * (a2[:HEAD_W] / a2[HEAD_W:HEAD_W + 1])
    ms = jnp.mean(ot * ot, axis=0, keepdims=True)
    ot = ot * (lax.rsqrt(ms + NORM_EPS) * (1.0 - lambda_init))
    o_ref[0] = (ot.T * ng_ref[...]).astype(o_ref.dtype)


def _diff_attn(q, k, vt, lq, lk, ng, lambda_init):
    bsz, t, _ = q.shape
    tq = ATT_TQ
    return pl.pallas_call(
        functools.partial(_diff_attn_kernel, lambda_init=lambda_init),
        grid=(bsz, N_HEADS, t // tq),
        in_specs=[
            pl.BlockSpec((1, tq, HEAD_W), lambda b, h, i: (b, i, h)),
            pl.BlockSpec((1, t, HEAD_W), lambda b, h, i: (b, 0, h)),
            pl.BlockSpec((1, 1, VT_ROWS, t), lambda b, h, i: (b, h, 0, 0)),
            _const_spec(lq.shape), _const_spec(lk.shape), _const_spec(ng.shape),
        ],
        out_specs=pl.BlockSpec((1, tq, HEAD_W), lambda b, h, i: (b, i, h)),
        out_shape=jax.ShapeDtypeStruct((bsz, t, D_MODEL), BF16),
        scratch_shapes=[pltpu.VMEM((2, 1, tq), F32), pltpu.VMEM((2, VT_ROWS, tq), F32),
                        pltpu.VMEM((2, ATT_TK, tq), F32), pltpu.VMEM((2, ATT_TK, tq), F32)],
        compiler_params=_params(("arbitrary", "arbitrary", "arbitrary")),
        name="diff_attn",
    )(q, k, vt, lq, lk, ng)


def kernel(x, ln_g, ln_b, ffn1_w_in, ffn1_w_out, ffn2_w_in, ffn2_w_out, gdn_w_in, gdn_conv_w, gdn_a_log,
           gdn_dt_bias, gdn_norm_g, gdn_w_out, diff_w_kv, diff_lambda_k, diff_w_q, diff_lambda_q, diff_norm_g,
           diff_w_out):
    bsz, t, d = x.shape
    n = bsz * t
    xs = x.reshape(n, d)
    w = 4 * D_MODEL

    def ffn(xs, w_in, w_out, g, b):
        return _ffn_ln(xs, w_in[:, :D_FF].astype(BF16), w_in[:, D_FF:].astype(BF16), w_out.astype(BF16),
                       g.reshape(1, d), b.reshape(1, d))

    k_sh = v_sh = None
    for l in range(DEPTH):
        xs = ffn(xs, ffn1_w_in[l], ffn1_w_out[l], ln_g[l, 0], ln_b[l, 0])
        if l < N_A:
            w_in = gdn_w_in[l]
            rep = lambda a: jnp.repeat(a, HEAD_W, axis=-1)
            q, k, v, gate, beta, gc = _gdn_in(
                xs.reshape(bsz, t, d),
                w_in[:, :3 * D_MODEL].astype(BF16), w_in[:, 3 * D_MODEL:w].astype(BF16),
                rep(w_in[:, w:w + N_HEADS]).astype(BF16), rep(w_in[:, w + N_HEADS:]).astype(BF16),
                gdn_conv_w[l], rep(gdn_a_log[l]).reshape(1, d), rep(gdn_dt_bias[l]).reshape(1, d))
            o = _gdn_chunk(q, k, v, gate, beta, gc, gdn_norm_g[l].reshape(1, HEAD_W))
            w_out = gdn_w_out[l]
        else:
            j = l - N_A
            lambda_init = 0.8 - 0.6 * math.exp(-0.3 * l)
            (q,) = _proj(xs, diff_w_q[j].astype(BF16), scale=DIFF_HEAD_DIM ** -0.5 * LOG2E)
            o = _diff_attn(q.reshape(bsz, t, d), k_sh, v_sh, diff_lambda_q[j], diff_lambda_k,
                           diff_norm_g[j].reshape(1, HEAD_W), lambda_init)
            w_out = diff_w_out[j]
        xs = _out_ln(o.reshape(n, d), xs, w_out.astype(BF16), ln_g[l, 1].reshape(1, d), ln_b[l, 1].reshape(1, d))
        xs = ffn(xs, ffn2_w_in[l], ffn2_w_out[l], ln_g[l, 2], ln_b[l, 2])
        if l == N_A - 1:
            k_sh, v_sh = _kv_proj(xs.reshape(bsz, t, d), diff_w_kv[:, :d].astype(BF16),
                                  diff_w_kv[:, d:].T.astype(BF16))
    return xs.reshape(bsz, t, d)
```

```python
import functools
import math

import jax
import jax.numpy as jnp
from jax import lax
from jax.experimental import pallas as pl
from jax.experimental.pallas import tpu as pltpu

F32 = jnp.float32
BF16 = jnp.bfloat16

D_MODEL = 1024
DEPTH = 4
N_A = DEPTH // 2
ALPHA = (2.0 * DEPTH) ** 0.25
LN_EPS = 1e-5
D_FF = 2816
HEAD_W = 128
N_HEADS = D_MODEL // HEAD_W
GDN_CONV = 4
GDN_CHUNK = 64
PAIR = 2 * GDN_CHUNK
DIFF_HEAD_DIM = 64
NORM_EPS = 1e-5
LOG2E = 1.4426950408889634

VMEM_LIMIT_V7X = 56 * 1024 * 1024

FFN_TM = 512
FFN_FC = 256
LIN_TM = 512
GDN_IN_TM = 256
GDN_TM = 512
GDN_GROUP = 2
ATT_TQ = 512
ATT_TK = ATT_TQ // 2
MASK_VALUE = -1e30


def _params(sem):
    return pltpu.CompilerParams(dimension_semantics=sem, vmem_limit_bytes=VMEM_LIMIT_V7X)


def _const_spec(shape):
    nd = len(shape)
    return pl.BlockSpec(shape, lambda *_: (0,) * nd, pipeline_mode=pl.Buffered(1))


def _layer_norm(y, g, b):
    mu = jnp.mean(y, axis=-1, keepdims=True)
    yc = y - mu
    var = jnp.mean(yc * yc, axis=-1, keepdims=True)
    return yc * lax.rsqrt(var + LN_EPS) * g + b


def _dot(a, b):
    return jnp.dot(a, b, preferred_element_type=F32)


def _dot_nt(a, b):
    return lax.dot_general(a, b, (((1,), (1,)), ((), ())), preferred_element_type=F32)


def _dot_tn(a, b):
    return lax.dot_general(a, b, (((0,), (0,)), ((), ())), preferred_element_type=F32)


def _silu(x):
    return x * jax.nn.sigmoid(x)


def _ffn_ln_kernel(*refs, with_mix):
    if with_mix:
        mix_ref, wm_ref, gm_ref, bm_ref, x_ref, wg_ref, wu_ref, wo_ref, g_ref, b_ref, o_ref, act_ref = refs
        x = _layer_norm(ALPHA * x_ref[...] + _dot(mix_ref[...], wm_ref[...]), gm_ref[...], bm_ref[...])
    else:
        x_ref, wg_ref, wu_ref, wo_ref, g_ref, b_ref, o_ref, act_ref = refs
        x = x_ref[...]
    xb = x.astype(BF16)
    for c in range(D_FF // FFN_FC):
        sl = slice(c * FFN_FC, (c + 1) * FFN_FC)
        gate = _dot(xb, wg_ref[:, sl])
        up = _dot(xb, wu_ref[:, sl])
        act_ref[:, sl] = (_silu(gate) * up).astype(BF16)
    h = _dot(act_ref[...], wo_ref[...])
    o_ref[...] = _layer_norm(ALPHA * x + 0.5 * h, g_ref[...], b_ref[...])


def _ffn_ln(x, wg, wu, wo, g, b, mix=None):
    n = x.shape[0]
    row_blk = pl.BlockSpec((FFN_TM, D_MODEL), lambda i: (i, 0))
    vec = _const_spec((1, D_MODEL))
    mix_specs = [] if mix is None else [row_blk, _const_spec((D_MODEL, D_MODEL)), vec, vec]
    return pl.pallas_call(
        functools.partial(_ffn_ln_kernel, with_mix=mix is not None),
        grid=(n // FFN_TM,),
        in_specs=mix_specs + [
            row_blk,
            _const_spec((D_MODEL, D_FF)),
            _const_spec((D_MODEL, D_FF)),
            _const_spec((D_FF, D_MODEL)),
            vec,
            vec,
        ],
        out_specs=row_blk,
        out_shape=jax.ShapeDtypeStruct((n, D_MODEL), F32),
        scratch_shapes=[pltpu.VMEM((FFN_TM, D_FF), BF16)],
        compiler_params=_params(("arbitrary",)),
        name="ffn_ln" if mix is None else "mix_ffn_ln",
    )(*(() if mix is None else mix), x, wg, wu, wo, g, b)


def _proj_kernel(x_ref, w_ref, *o_refs, scale):
    xb = x_ref[...].astype(BF16)
    for j, o_ref in enumerate(o_refs):
        y = _dot(xb, w_ref[:, j * D_MODEL:(j + 1) * D_MODEL])
        if scale != 1.0:
            y = y * scale
        o_ref[...] = y.astype(o_ref.dtype)


def _proj(x, w, scale=1.0):
    n = x.shape[0]
    n_out = w.shape[1] // D_MODEL
    outs = pl.pallas_call(
        functools.partial(_proj_kernel, scale=scale),
        grid=(n // LIN_TM,),
        in_specs=[
            pl.BlockSpec((LIN_TM, D_MODEL), lambda i: (i, 0)),
            _const_spec(w.shape),
        ],
        out_specs=[pl.BlockSpec((LIN_TM, D_MODEL), lambda i: (i, 0))] * n_out,
        out_shape=[jax.ShapeDtypeStruct((n, D_MODEL), BF16)] * n_out,
        compiler_params=_params(("arbitrary",)),
        name="proj",
    )(x, w)
    return outs


def _split3(x):
    x1 = x.astype(BF16)
    r1 = x - x1.astype(F32)
    x2 = r1.astype(BF16)
    x3 = (r1 - x2.astype(F32)).astype(BF16)
    return x1, x2, x3


def _gdn_in_kernel(x_ref, wqkv_ref, wz_ref, wba_ref, cw_ref, alog_ref, dtb_ref,
                   q_ref, k_ref, v_ref, gate_ref, beta_ref, gc_ref, ext_ref):
    tm = GDN_IN_TM
    w3 = 3 * D_MODEL

    @pl.when(pl.program_id(1) == 0)
    def _():
        ext_ref[0:8, :] = jnp.zeros((8, w3), F32)

    xb = x_ref[0].astype(BF16)
    ext_ref[8:8 + tm, :] = _dot(xb, wqkv_ref[...])

    outs = (q_ref, k_ref, v_ref)
    for cb in range(3 * N_HEADS):
        sl = slice(cb * HEAD_W, (cb + 1) * HEAD_W)
        y = cw_ref[3:4, sl] * ext_ref[8:8 + tm, sl]
        for j in range(GDN_CONV - 1):
            y = y + cw_ref[j:j + 1, sl] * ext_ref[5 + j:5 + j + tm, sl]
        y = _silu(y)
        which, h = divmod(cb, N_HEADS)
        if which < 2:
            ss = jnp.sum(y * y, axis=-1, keepdims=True)
            y = y * lax.rsqrt(ss + 1e-6)
            if which == 0:
                y = y * (HEAD_W ** -0.5)
        outs[which][0, :, h * HEAD_W:(h + 1) * HEAD_W] = y

    ext_ref[0:8, :] = ext_ref[tm:tm + 8, :]

    gate_ref[0] = _silu(_dot(xb, wz_ref[...]))
    ba = _dot(xb, wba_ref[...])
    beta = jax.nn.sigmoid(ba)
    a = ba + dtb_ref[...]
    softplus = jnp.maximum(a, 0.0) + jnp.log(1.0 + jnp.exp(-jnp.abs(a)))
    g = -jnp.exp(alog_ref[...]) * softplus
    row = lax.broadcasted_iota(jnp.int32, (tm, tm), 0)
    col = lax.broadcasted_iota(jnp.int32, (tm, tm), 1)
    tri = ((row // GDN_CHUNK == col // GDN_CHUNK) & (col <= row)).astype(BF16)
    g1, g2, g3 = _split3(g)
    gc = _dot(tri, g1) + _dot(tri, g2) + _dot(tri, g3)
    for h in range(N_HEADS):
        sl = slice(h * HEAD_W, (h + 1) * HEAD_W)
        beta_ref[0, :, sl] = jnp.broadcast_to(beta[:, h:h + 1], (tm, HEAD_W))
        gc_ref[0, :, sl] = jnp.broadcast_to(gc[:, N_HEADS + h:N_HEADS + h + 1], (tm, HEAD_W))


def _gdn_in(x3, wqkv, wz, wba, cw, alog, dtb):
    bsz, t, _ = x3.shape
    tm = GDN_IN_TM
    blk = pl.BlockSpec((1, tm, D_MODEL), lambda b, i: (b, i, 0))
    return pl.pallas_call(
        _gdn_in_kernel,
        grid=(bsz, t // tm),
        in_specs=[
            blk,
            _const_spec(wqkv.shape), _const_spec(wz.shape), _const_spec(wba.shape),
            _const_spec(cw.shape), _const_spec(alog.shape), _const_spec(dtb.shape),
        ],
        out_specs=[blk] * 6,
        out_shape=[jax.ShapeDtypeStruct((bsz, t, D_MODEL), F32)] * 6,
        scratch_shapes=[pltpu.VMEM((tm + 8, 3 * D_MODEL), F32)],
        compiler_params=_params(("arbitrary", "arbitrary")),
        name="gdn_in",
    )(x3, wqkv, wz, wba, cw, alog, dtb)


def _mm(a, b):
    return _dot(a.astype(BF16), b.astype(BF16))


def _gdn_chunk_kernel(q_ref, k_ref, v_ref, gate_ref, beta_ref, gc_ref, ng_ref, o_ref, s_ref):
    @pl.when(pl.program_id(1) == 0)
    def _():
        s_ref[...] = jnp.zeros_like(s_ref)

    row = lax.broadcasted_iota(jnp.int32, (PAIR, PAIR), 0)
    col = lax.broadcasted_iota(jnp.int32, (PAIR, PAIR), 1)
    same64 = (row // 64) == (col // 64)
    causal = same64 & (col <= row)
    strict = same64 & (col < row)
    same32 = (row // 32) == (col // 32)
    same16 = (row // 16) == (col // 16)
    m16 = (strict & same16).astype(F32)
    m32 = (strict & same32 & ~same16).astype(F32)
    m64 = (strict & ~same32).astype(F32)
    first = lax.broadcasted_iota(jnp.int32, (PAIR, HEAD_W), 0) < GDN_CHUNK
    ng = ng_ref[...]
    heads = range(N_HEADS)

    def each(fn, *lists):
        return [fn(*args) for args in zip(*lists)]

    def group_body(it, carry):
        base = it * (GDN_GROUP * PAIR)
        rows = [pl.ds(pl.multiple_of(base + g * PAIR, PAIR), PAIR) for g in range(GDN_GROUP)]
        units = [(g, h) for g in range(GDN_GROUP) for h in heads]

        def load(ref):
            return [ref[0, rows[g], h * HEAD_W:(h + 1) * HEAD_W] for g, h in units]

        q, k, v, beta, gc = load(q_ref), load(k_ref), load(v_ref), load(beta_ref), load(gc_ref)

        def decay_of(g):
            d = jnp.where(causal, g - g.T, 0.0)
            return jnp.where(causal, jnp.exp(d), 0.0)

        decay = each(decay_of, gc)
        kb = each(lambda a, b: a * b, k, beta)
        kbf = each(lambda a: a.astype(BF16), k)
        a_mat = each(lambda a, b, c: _dot_nt(a.astype(BF16), b) * c, kb, kbf, decay)
        attn = each(lambda a, b, c: _dot_nt(a.astype(BF16), b) * c, q, kbf, decay)
        egc = each(jnp.exp, gc)

        n1 = each(lambda a: -(a * m16), a_mat)
        n2 = each(_mm, n1, n1)
        x = each(lambda a, b, c: a + b + c, n1, n2, each(_mm, n1, n2))
        n4 = each(_mm, n2, n2)
        x = each(lambda a, b, c: a + b + c, x, n4, each(_mm, x, n4))
        n8 = each(_mm, n4, n4)
        x = each(lambda a, b, c: a + b + c, x, n8, each(_mm, x, n8))
        for mask in (m32, m64):
            low = each(lambda a: a * mask, a_mat)
            t = each(lambda a, b: a + b, low, each(_mm, low, x))
            x = each(lambda a, b, c: a - b - c, x, t, each(_mm, x, t))

        rhs = each(lambda a, b, c, d: jnp.concatenate([a * b, c * d], axis=1), kb, egc, v, beta)
        wu = each(lambda a, b: b + _mm(a, b), x, rhs)
        w = each(lambda a: a[:, :HEAD_W].astype(BF16), wu)
        u = each(lambda a: a[:, HEAD_W:], wu)

        gl = each(lambda g: jnp.where(first, g[GDN_CHUNK - 1:GDN_CHUNK, :], g[PAIR - 1:PAIR, :]), gc)
        kd = each(lambda a, b, c: (a * jnp.exp(b - c)).astype(BF16), k, gl, gc)
        qd = each(lambda a, b: (a * b).astype(BF16), q, egc)
        cd = each(jnp.exp, gl)
        gate = load(gate_ref)

        s = [s_ref[h] for h in heads]
        for g in range(GDN_GROUP):
            of_pair = lambda lst: lst[g * N_HEADS:(g + 1) * N_HEADS]
            u_g, w_g, qd_g, kd_g, cd_g = of_pair(u), of_pair(w), of_pair(qd), of_pair(kd), of_pair(cd)
            vns, qss = [], []
            for c in range(2):
                rs = slice(c * GDN_CHUNK, (c + 1) * GDN_CHUNK)
                cd_row = slice(c * GDN_CHUNK, c * GDN_CHUNK + 1)
                sb = each(lambda a: a.astype(BF16), s)
                vn = each(lambda a, b, c2: a[rs] - _dot(b[rs], c2), u_g, w_g, sb)
                qss.append(each(lambda a, b: _dot(a[rs], b), qd_g, sb))
                s = each(lambda a, b, c2, d: a * b[cd_row, :] + _dot_tn(c2[rs], d.astype(BF16)), s, cd_g, kd_g, vn)
                vns.append(vn)
            vn = each(lambda a, b: jnp.concatenate([a, b], axis=0), vns[0], vns[1])
            qs = each(lambda a, b: jnp.concatenate([a, b], axis=0), qss[0], qss[1])
            o = each(lambda a, b, c: a + _mm(b, c), qs, of_pair(attn), vn)
            gate_g = of_pair(gate)
            for h in heads:
                ms = jnp.mean(o[h] * o[h], axis=-1, keepdims=True)
                oh = o[h] * lax.rsqrt(ms + NORM_EPS) * ng * gate_g[h]
                o_ref[0, rows[g], h * HEAD_W:(h + 1) * HEAD_W] = oh.astype(o_ref.dtype)
        for h in heads:
            s_ref[h] = s[h]
        return carry

    lax.fori_loop(0, GDN_TM // (GDN_GROUP * PAIR), group_body, 0)


def _gdn_chunk(q, k, v, gate, beta, gc, ng):
    bsz, t, _ = q.shape
    tm = GDN_TM
    blk = pl.BlockSpec((1, tm, D_MODEL), lambda b, i: (b, i, 0))
    return pl.pallas_call(
        _gdn_chunk_kernel,
        grid=(bsz, t // tm),
        in_specs=[blk] * 6 + [_const_spec(ng.shape)],
        out_specs=blk,
        out_shape=jax.ShapeDtypeStruct((bsz, t, D_MODEL), BF16),
        scratch_shapes=[pltpu.VMEM((N_HEADS, HEAD_W, HEAD_W), F32)],
        compiler_params=_params(("arbitrary", "arbitrary")),
        name="gdn_chunk",
    )(q, k, v, gate, beta, gc, ng)


VT_ROWS = HEAD_W + 16


def _kv_proj_kernel(x_ref, wk_ref, wvt_ref, k_ref, vt_ref):
    xb = x_ref[0].astype(BF16)
    k_ref[0] = _dot(xb, wk_ref[...]).astype(BF16)
    vt = _dot_nt(wvt_ref[...], xb).astype(BF16)
    tm = vt.shape[1]
    for h in range(N_HEADS):
        vt_ref[0, h, 0:HEAD_W, :] = vt[h * HEAD_W:(h + 1) * HEAD_W, :]
        vt_ref[0, h, HEAD_W:VT_ROWS, :] = jnp.ones((VT_ROWS - HEAD_W, tm), BF16)


def _kv_proj(x3, wk, wvt):
    bsz, t, _ = x3.shape
    tm = LIN_TM
    return pl.pallas_call(
        _kv_proj_kernel,
        grid=(bsz, t // tm),
        in_specs=[
            pl.BlockSpec((1, tm, D_MODEL), lambda b, i: (b, i, 0)),
            _const_spec(wk.shape), _const_spec(wvt.shape),
        ],
        out_specs=[
            pl.BlockSpec((1, tm, D_MODEL), lambda b, i: (b, i, 0)),
            pl.BlockSpec((1, N_HEADS, VT_ROWS, tm), lambda b, i: (b, 0, 0, i)),
        ],
        out_shape=[
            jax.ShapeDtypeStruct((bsz, t, D_MODEL), BF16),
            jax.ShapeDtypeStruct((bsz, N_HEADS, VT_ROWS, t), BF16),
        ],
        compiler_params=_params(("arbitrary", "arbitrary")),
        name="kv_proj",
    )(x3, wk, wvt)


def _diff_attn_kernel(q_ref, k_ref, vt_ref, lq_ref, lk_ref, ng_ref, o_ref, m_ref, acc_ref, sa_ref, sb_ref, *,
                      lambda_init):
    tq, tk = ATT_TQ, ATT_TK
    i = pl.program_id(2)
    q = q_ref[0]
    lane = lax.broadcasted_iota(jnp.int32, (tq, HEAD_W), 1)
    zero = jnp.zeros_like(q)
    qs = (jnp.where(lane < DIFF_HEAD_DIM, q, zero), jnp.where(lane >= DIFF_HEAD_DIM, q, zero))

    m_ref[...] = jnp.full(m_ref.shape, MASK_VALUE, F32)
    acc_ref[...] = jnp.zeros(acc_ref.shape, F32)

    def key_offset(n):
        t = jnp.where(n < 2, 2 * i + n, n - 2)
        return pl.multiple_of(t * tk, tk)

    def scores(n, dst_ref):
        k = k_ref[0, pl.ds(key_offset(n), tk), :]
        for c in range(2):
            dst_ref[c] = _dot_nt(k, qs[c])

    def consume(n, src_ref, masked):
        off = key_offset(n)
        vt = vt_ref[0, 0, :, pl.ds(off, tk)]
        for c in range(2):
            st = src_ref[c]
            if masked:
                kpos = off + lax.broadcasted_iota(jnp.int32, (tk, tq), 0)
                qpos = i * tq + lax.broadcasted_iota(jnp.int32, (tk, tq), 1)
                st = jnp.where(kpos <= qpos, st, MASK_VALUE)
            m_old = m_ref[c]
            m_new = jnp.maximum(m_old, jnp.max(st, axis=0, keepdims=True))
            alpha = jnp.exp2(m_old - m_new)
            p = jnp.exp2(st - m_new).astype(BF16)
            acc_ref[c] = alpha * acc_ref[c] + _dot(vt, p)
            m_ref[c] = m_new

    scores(0, sa_ref)
    scores(1, sb_ref)
    consume(0, sa_ref, True)
    scores(2, sa_ref)
    consume(1, sb_ref, True)

    def pair(n):
        scores(n + 1, sb_ref)
        consume(n, sa_ref, False)
        scores(n + 2, sa_ref)
        consume(n + 1, sb_ref, False)

    def body(qq, carry):
        pair(4 * qq + 2)
        pair(4 * qq + 4)
        return carry

    lax.fori_loop(0, lax.shift_right_logical(i, 1), body, 0)

    @pl.when((i & 1) == 1)
    def _():
        pair(2 * i)

    lqk = lq_ref[...] * lk_ref[...]
    lam = (jnp.exp(jnp.sum(lqk[0:1, :], axis=-1, keepdims=True))
           - jnp.exp(jnp.sum(lqk[1:2, :], axis=-1, keepdims=True)) + lambda_init)
    a1 = acc_ref[0]
    a2 = acc_ref[1]
    ot = a1[:HEAD_W] / a1[HEAD_W:HEAD_W + 1] - lam * (a2[:HEAD_W] / a2[HEAD_W:HEAD_W + 1])
    ms = jnp.mean(ot * ot, axis=0, keepdims=True)
    ot = ot * (lax.rsqrt(ms + NORM_EPS) * (1.0 - lambda_init))
    o_ref[0] = (ot.T * ng_ref[...]).astype(o_ref.dtype)


def _diff_attn(q, k, vt, lq, lk, ng, lambda_init):
    bsz, t, _ = q.shape
    tq = ATT_TQ
    return pl.pallas_call(
        functools.partial(_diff_attn_kernel, lambda_init=lambda_init),
        grid=(bsz, N_HEADS, t // tq),
        in_specs=[
            pl.BlockSpec((1, tq, HEAD_W), lambda b, h, i: (b, i, h)),
            pl.BlockSpec((1, t, HEAD_W), lambda b, h, i: (b, 0, h)),
            pl.BlockSpec((1, 1, VT_ROWS, t), lambda b, h, i: (b, h, 0, 0)),
            _const_spec(lq.shape), _const_spec(lk.shape), _const_spec(ng.shape),
        ],
        out_specs=pl.BlockSpec((1, tq, HEAD_W), lambda b, h, i: (b, i, h)),
        out_shape=jax.ShapeDtypeStruct((bsz, t, D_MODEL), BF16),
        scratch_shapes=[pltpu.VMEM((2, 1, tq), F32), pltpu.VMEM((2, VT_ROWS, tq), F32),
                        pltpu.VMEM((2, ATT_TK, tq), F32), pltpu.VMEM((2, ATT_TK, tq), F32)],
        compiler_params=_params(("arbitrary", "arbitrary", "arbitrary")),
        name="diff_attn",
    )(q, k, vt, lq, lk, ng)


def kernel(x, ln_g, ln_b, ffn1_w_in, ffn1_w_out, ffn2_w_in, ffn2_w_out, gdn_w_in, gdn_conv_w, gdn_a_log,
           gdn_dt_bias, gdn_norm_g, gdn_w_out, diff_w_kv, diff_lambda_k, diff_w_q, diff_lambda_q, diff_norm_g,
           diff_w_out):
    bsz, t, d = x.shape
    n = bsz * t
    xs = x.reshape(n, d)
    w = 4 * D_MODEL

    def ffn(xs, w_in, w_out, g, b, mix=None):
        return _ffn_ln(xs, w_in[:, :D_FF].astype(BF16), w_in[:, D_FF:].astype(BF16), w_out.astype(BF16),
                       g.reshape(1, d), b.reshape(1, d), mix=mix)

    def pad_lanes(a, lo):
        return jnp.pad(a, [(0, 0)] * (a.ndim - 1) + [(lo, HEAD_W - lo - a.shape[-1])])

    k_sh = v_sh = None
    for l in range(DEPTH):
        xs = ffn(xs, ffn1_w_in[l], ffn1_w_out[l], ln_g[l, 0], ln_b[l, 0])
        if l < N_A:
            w_in = gdn_w_in[l]
            q, k, v, gate, beta, gc = _gdn_in(
                xs.reshape(bsz, t, d),
                w_in[:, :3 * D_MODEL].astype(BF16), w_in[:, 3 * D_MODEL:w].astype(BF16),
                pad_lanes(w_in[:, w:], 0).astype(BF16),
                gdn_conv_w[l], pad_lanes(gdn_a_log[l].reshape(1, N_HEADS), N_HEADS),
                pad_lanes(gdn_dt_bias[l].reshape(1, N_HEADS), N_HEADS))
            o = _gdn_chunk(q, k, v, gate, beta, gc, gdn_norm_g[l].reshape(1, HEAD_W))
            w_out = gdn_w_out[l]
        else:
            j = l - N_A
            lambda_init = 0.8 - 0.6 * math.exp(-0.3 * l)
            (q,) = _proj(xs, diff_w_q[j].astype(BF16), scale=DIFF_HEAD_DIM ** -0.5 * LOG2E)
            o = _diff_attn(q.reshape(bsz, t, d), k_sh, v_sh, diff_lambda_q[j], diff_lambda_k,
                           diff_norm_g[j].reshape(1, HEAD_W), lambda_init)
            w_out = diff_w_out[j]
        mix = (o.reshape(n, d), w_out.astype(BF16), ln_g[l, 1].reshape(1, d), ln_b[l, 1].reshape(1, d))
        xs = ffn(xs, ffn2_w_in[l], ffn2_w_out[l], ln_g[l, 2], ln_b[l, 2], mix=mix)
        if l == N_A - 1:
            k_sh, v_sh = _kv_proj(xs.reshape(bsz, t, d), diff_w_kv[:, :d].astype(BF16),
                                  diff_w_kv[:, d:].T.astype(BF16))
    return xs.reshape(bsz, t, d)
```

```python
import functools
import math

import jax
import jax.numpy as jnp
from jax import lax
from jax.experimental import pallas as pl
from jax.experimental.pallas import tpu as pltpu

F32 = jnp.float32
BF16 = jnp.bfloat16

D_MODEL = 1024
DEPTH = 4
N_A = DEPTH // 2
ALPHA = (2.0 * DEPTH) ** 0.25
LN_EPS = 1e-5
D_FF = 2816
HEAD_W = 128
N_HEADS = D_MODEL // HEAD_W
GDN_CONV = 4
GDN_CHUNK = 64
PAIR = 2 * GDN_CHUNK
DIFF_HEAD_DIM = 64
NORM_EPS = 1e-5
LOG2E = 1.4426950408889634

VMEM_LIMIT_V7X = 56 * 1024 * 1024

FFN_TM = 512
FFN_FC = 256
LIN_TM = 512
GDN_IN_TM = 256
GDN_TM = 512
GDN_GROUP = 2
ATT_TQ = 512
ATT_TK = ATT_TQ // 2
MASK_VALUE = -1e30


def _params(sem):
    return pltpu.CompilerParams(dimension_semantics=sem, vmem_limit_bytes=VMEM_LIMIT_V7X)


def _const_spec(shape):
    nd = len(shape)
    return pl.BlockSpec(shape, lambda *_: (0,) * nd, pipeline_mode=pl.Buffered(1))


def _layer_norm(y, g, b):
    mu = jnp.mean(y, axis=-1, keepdims=True)
    yc = y - mu
    var = jnp.mean(yc * yc, axis=-1, keepdims=True)
    return yc * lax.rsqrt(var + LN_EPS) * g + b


def _dot(a, b):
    return jnp.dot(a, b, preferred_element_type=F32)


def _dot_nt(a, b):
    return lax.dot_general(a, b, (((1,), (1,)), ((), ())), preferred_element_type=F32)


def _dot_tn(a, b):
    return lax.dot_general(a, b, (((0,), (0,)), ((), ())), preferred_element_type=F32)


def _silu(x):
    return x * jax.nn.sigmoid(x)


def _ffn_ln_kernel(*refs, with_mix):
    if with_mix:
        mix_ref, wm_ref, gm_ref, bm_ref, x_ref, wg_ref, wu_ref, wo_ref, g_ref, b_ref, o_ref, act_ref = refs
        x = _layer_norm(ALPHA * x_ref[...] + _dot(mix_ref[...], wm_ref[...]), gm_ref[...], bm_ref[...])
    else:
        x_ref, wg_ref, wu_ref, wo_ref, g_ref, b_ref, o_ref, act_ref = refs
        x = x_ref[...]
    xb = x.astype(BF16)
    for c in range(D_FF // FFN_FC):
        sl = slice(c * FFN_FC, (c + 1) * FFN_FC)
        gate = _dot(xb, wg_ref[:, sl])
        up = _dot(xb, wu_ref[:, sl])
        act_ref[:, sl] = (_silu(gate) * up).astype(BF16)
    h = _dot(act_ref[...], wo_ref[...])
    o_ref[...] = _layer_norm(ALPHA * x + 0.5 * h, g_ref[...], b_ref[...])


def _ffn_ln(x, wg, wu, wo, g, b, mix=None):
    n = x.shape[0]
    row_blk = pl.BlockSpec((FFN_TM, D_MODEL), lambda i: (i, 0))
    vec = _const_spec((1, D_MODEL))
    mix_specs = [] if mix is None else [row_blk, _const_spec((D_MODEL, D_MODEL)), vec, vec]
    return pl.pallas_call(
        functools.partial(_ffn_ln_kernel, with_mix=mix is not None),
        grid=(n // FFN_TM,),
        in_specs=mix_specs + [
            row_blk,
            _const_spec((D_MODEL, D_FF)),
            _const_spec((D_MODEL, D_FF)),
            _const_spec((D_FF, D_MODEL)),
            vec,
            vec,
        ],
        out_specs=row_blk,
        out_shape=jax.ShapeDtypeStruct((n, D_MODEL), F32),
        scratch_shapes=[pltpu.VMEM((FFN_TM, D_FF), BF16)],
        compiler_params=_params(("arbitrary",)),
        name="ffn_ln" if mix is None else "mix_ffn_ln",
    )(*(() if mix is None else mix), x, wg, wu, wo, g, b)


def _proj_kernel(x_ref, w_ref, *o_refs, scale):
    xb = x_ref[...].astype(BF16)
    for j, o_ref in enumerate(o_refs):
        y = _dot(xb, w_ref[:, j * D_MODEL:(j + 1) * D_MODEL])
        if scale != 1.0:
            y = y * scale
        o_ref[...] = y.astype(o_ref.dtype)


def _proj(x, w, scale=1.0):
    n = x.shape[0]
    n_out = w.shape[1] // D_MODEL
    outs = pl.pallas_call(
        functools.partial(_proj_kernel, scale=scale),
        grid=(n // LIN_TM,),
        in_specs=[
            pl.BlockSpec((LIN_TM, D_MODEL), lambda i: (i, 0)),
            _const_spec(w.shape),
        ],
        out_specs=[pl.BlockSpec((LIN_TM, D_MODEL), lambda i: (i, 0))] * n_out,
        out_shape=[jax.ShapeDtypeStruct((n, D_MODEL), BF16)] * n_out,
        compiler_params=_params(("arbitrary",)),
        name="proj",
    )(x, w)
    return outs


def _split3(x):
    x1 = x.astype(BF16)
    r1 = x - x1.astype(F32)
    x2 = r1.astype(BF16)
    x3 = (r1 - x2.astype(F32)).astype(BF16)
    return x1, x2, x3


def _gdn_in_kernel(x_ref, wqkv_ref, wz_ref, wba_ref, cw_ref, alog_ref, dtb_ref,
                   q_ref, k_ref, v_ref, gate_ref, beta_ref, gc_ref, ext_ref):
    tm = GDN_IN_TM
    w3 = 3 * D_MODEL

    @pl.when(pl.program_id(1) == 0)
    def _():
        ext_ref[0:8, :] = jnp.zeros((8, w3), F32)

    xb = x_ref[0].astype(BF16)

    ba = _dot(xb, wba_ref[...])
    beta = jax.nn.sigmoid(ba)
    a = ba + dtb_ref[...]
    softplus = jnp.maximum(a, 0.0) + jnp.log(1.0 + jnp.exp(-jnp.abs(a)))
    g = -jnp.exp(alog_ref[...]) * softplus
    row = lax.broadcasted_iota(jnp.int32, (tm, tm), 0)
    col = lax.broadcasted_iota(jnp.int32, (tm, tm), 1)
    tri = ((row // GDN_CHUNK == col // GDN_CHUNK) & (col <= row)).astype(BF16)
    g1, g2, g3 = _split3(g)
    gc = _dot(tri, g1) + _dot(tri, g2) + _dot(tri, g3)
    for h in range(N_HEADS):
        sl = slice(h * HEAD_W, (h + 1) * HEAD_W)
        beta_ref[0, :, sl] = jnp.broadcast_to(beta[:, h:h + 1], (tm, HEAD_W))
        gc_ref[0, :, sl] = jnp.broadcast_to(gc[:, N_HEADS + h:N_HEADS + h + 1], (tm, HEAD_W))
    gate_ref[0] = _silu(_dot(xb, wz_ref[...]))

    ext_ref[8:8 + tm, :] = _dot(xb, wqkv_ref[...])
    outs = (q_ref, k_ref, v_ref)
    for cb in range(3 * N_HEADS):
        sl = slice(cb * HEAD_W, (cb + 1) * HEAD_W)
        y = cw_ref[3:4, sl] * ext_ref[8:8 + tm, sl]
        for j in range(GDN_CONV - 1):
            y = y + cw_ref[j:j + 1, sl] * ext_ref[5 + j:5 + j + tm, sl]
        y = _silu(y)
        which, h = divmod(cb, N_HEADS)
        if which < 2:
            ss = jnp.sum(y * y, axis=-1, keepdims=True)
            y = y * lax.rsqrt(ss + 1e-6)
            if which == 0:
                y = y * (HEAD_W ** -0.5)
        outs[which][0, :, h * HEAD_W:(h + 1) * HEAD_W] = y

    ext_ref[0:8, :] = ext_ref[tm:tm + 8, :]


def _gdn_in(x3, wqkv, wz, wba, cw, alog, dtb):
    bsz, t, _ = x3.shape
    tm = GDN_IN_TM
    blk = pl.BlockSpec((1, tm, D_MODEL), lambda b, i: (b, i, 0))
    return pl.pallas_call(
        _gdn_in_kernel,
        grid=(bsz, t // tm),
        in_specs=[
            blk,
            _const_spec(wqkv.shape), _const_spec(wz.shape), _const_spec(wba.shape),
            _const_spec(cw.shape), _const_spec(alog.shape), _const_spec(dtb.shape),
        ],
        out_specs=[blk] * 6,
        out_shape=[jax.ShapeDtypeStruct((bsz, t, D_MODEL), F32)] * 6,
        scratch_shapes=[pltpu.VMEM((tm + 8, 3 * D_MODEL), F32)],
        compiler_params=_params(("arbitrary", "arbitrary")),
        name="gdn_in",
    )(x3, wqkv, wz, wba, cw, alog, dtb)


def _mm(a, b):
    return _dot(a.astype(BF16), b.astype(BF16))


def _gdn_chunk_kernel(q_ref, k_ref, v_ref, gate_ref, beta_ref, gc_ref, ng_ref, o_ref, s_ref):
    @pl.when(pl.program_id(1) == 0)
    def _():
        s_ref[...] = jnp.zeros_like(s_ref)

    row = lax.broadcasted_iota(jnp.int32, (PAIR, PAIR), 0)
    col = lax.broadcasted_iota(jnp.int32, (PAIR, PAIR), 1)
    same64 = (row // 64) == (col // 64)
    causal = same64 & (col <= row)
    strict = same64 & (col < row)
    same32 = (row // 32) == (col // 32)
    same16 = (row // 16) == (col // 16)
    m16 = (strict & same16).astype(F32)
    m32 = (strict & same32 & ~same16).astype(F32)
    m64 = (strict & ~same32).astype(F32)
    first = lax.broadcasted_iota(jnp.int32, (PAIR, HEAD_W), 0) < GDN_CHUNK
    ng = ng_ref[...]
    heads = range(N_HEADS)

    def each(fn, *lists):
        return [fn(*args) for args in zip(*lists)]

    def group_body(it, carry):
        base = it * (GDN_GROUP * PAIR)
        rows = [pl.ds(pl.multiple_of(base + g * PAIR, PAIR), PAIR) for g in range(GDN_GROUP)]
        units = [(g, h) for g in range(GDN_GROUP) for h in heads]

        def load(ref):
            return [ref[0, rows[g], h * HEAD_W:(h + 1) * HEAD_W] for g, h in units]

        q, k, v, beta, gc = load(q_ref), load(k_ref), load(v_ref), load(beta_ref), load(gc_ref)

        def decay_of(g):
            d = jnp.where(causal, g - g.T, 0.0)
            return jnp.where(causal, jnp.exp(d), 0.0)

        decay = each(decay_of, gc)
        kb = each(lambda a, b: a * b, k, beta)
        kbf = each(lambda a: a.astype(BF16), k)
        a_mat = each(lambda a, b, c: _dot_nt(a.astype(BF16), b) * c, kb, kbf, decay)
        attn = each(lambda a, b, c: _dot_nt(a.astype(BF16), b) * c, q, kbf, decay)
        egc = each(jnp.exp, gc)

        n1 = each(lambda a: -(a * m16), a_mat)
        n2 = each(_mm, n1, n1)
        x = each(lambda a, b, c: a + b + c, n1, n2, each(_mm, n1, n2))
        n4 = each(_mm, n2, n2)
        x = each(lambda a, b, c: a + b + c, x, n4, each(_mm, x, n4))
        n8 = each(_mm, n4, n4)
        x = each(lambda a, b, c: a + b + c, x, n8, each(_mm, x, n8))
        for mask in (m32, m64):
            low = each(lambda a: a * mask, a_mat)
            t = each(lambda a, b: a + b, low, each(_mm, low, x))
            x = each(lambda a, b, c: a - b - c, x, t, each(_mm, x, t))

        rhs = each(lambda a, b, c, d: jnp.concatenate([a * b, c * d], axis=1), kb, egc, v, beta)
        wu = each(lambda a, b: b + _mm(a, b), x, rhs)
        w = each(lambda a: a[:, :HEAD_W].astype(BF16), wu)
        u = each(lambda a: a[:, HEAD_W:], wu)

        gl = each(lambda g: jnp.where(first, g[GDN_CHUNK - 1:GDN_CHUNK, :], g[PAIR - 1:PAIR, :]), gc)
        kd = each(lambda a, b, c: (a * jnp.exp(b - c)).astype(BF16), k, gl, gc)
        qd = each(lambda a, b: (a * b).astype(BF16), q, egc)
        cd = each(jnp.exp, gl)
        gate = load(gate_ref)

        s = [s_ref[h] for h in heads]
        for g in range(GDN_GROUP):
            of_pair = lambda lst: lst[g * N_HEADS:(g + 1) * N_HEADS]
            u_g, w_g, qd_g, kd_g, cd_g = of_pair(u), of_pair(w), of_pair(qd), of_pair(kd), of_pair(cd)
            vns, qss = [], []
            for c in range(2):
                rs = slice(c * GDN_CHUNK, (c + 1) * GDN_CHUNK)
                cd_row = slice(c * GDN_CHUNK, c * GDN_CHUNK + 1)
                sb = each(lambda a: a.astype(BF16), s)
                vn = each(lambda a, b, c2: a[rs] - _dot(b[rs], c2), u_g, w_g, sb)
                qss.append(each(lambda a, b: _dot(a[rs], b), qd_g, sb))
                s = each(lambda a, b, c2, d: a * b[cd_row, :] + _dot_tn(c2[rs], d.astype(BF16)), s, cd_g, kd_g, vn)
                vns.append(vn)
            vn = each(lambda a, b: jnp.concatenate([a, b], axis=0), vns[0], vns[1])
            qs = each(lambda a, b: jnp.concatenate([a, b], axis=0), qss[0], qss[1])
            o = each(lambda a, b, c: a + _mm(b, c), qs, of_pair(attn), vn)
            gate_g = of_pair(gate)
            for h in heads:
                ms = jnp.mean(o[h] * o[h], axis=-1, keepdims=True)
                oh = o[h] * lax.rsqrt(ms + NORM_EPS) * ng * gate_g[h]
                o_ref[0, rows[g], h * HEAD_W:(h + 1) * HEAD_W] = oh.astype(o_ref.dtype)
        for h in heads:
            s_ref[h] = s[h]
        return carry

    lax.fori_loop(0, GDN_TM // (GDN_GROUP * PAIR), group_body, 0)


def _gdn_chunk(q, k, v, gate, beta, gc, ng):
    bsz, t, _ = q.shape
    tm = GDN_TM
    blk = pl.BlockSpec((1, tm, D_MODEL), lambda b, i: (b, i, 0))
    return pl.pallas_call(
        _gdn_chunk_kernel,
        grid=(bsz, t // tm),
        in_specs=[blk] * 6 + [_const_spec(ng.shape)],
        out_specs=blk,
        out_shape=jax.ShapeDtypeStruct((bsz, t, D_MODEL), BF16),
        scratch_shapes=[pltpu.VMEM((N_HEADS, HEAD_W, HEAD_W), F32)],
        compiler_params=_params(("arbitrary", "arbitrary")),
        name="gdn_chunk",
    )(q, k, v, gate, beta, gc, ng)


VT_ROWS = HEAD_W + 16


def _kv_proj_kernel(x_ref, wk_ref, wvt_ref, k_ref, vt_ref):
    xb = x_ref[0].astype(BF16)
    k_ref[0] = _dot(xb, wk_ref[...]).astype(BF16)
    vt = _dot_nt(wvt_ref[...], xb).astype(BF16)
    tm = vt.shape[1]
    for h in range(N_HEADS):
        vt_ref[0, h, 0:HEAD_W, :] = vt[h * HEAD_W:(h + 1) * HEAD_W, :]
        vt_ref[0, h, HEAD_W:VT_ROWS, :] = jnp.ones((VT_ROWS - HEAD_W, tm), BF16)


def _kv_proj(x3, wk, wvt):
    bsz, t, _ = x3.shape
    tm = LIN_TM
    return pl.pallas_call(
        _kv_proj_kernel,
        grid=(bsz, t // tm),
        in_specs=[
            pl.BlockSpec((1, tm, D_MODEL), lambda b, i: (b, i, 0)),
            _const_spec(wk.shape), _const_spec(wvt.shape),
        ],
        out_specs=[
            pl.BlockSpec((1, tm, D_MODEL), lambda b, i: (b, i, 0)),
            pl.BlockSpec((1, N_HEADS, VT_ROWS, tm), lambda b, i: (b, 0, 0, i)),
        ],
        out_shape=[
            jax.ShapeDtypeStruct((bsz, t, D_MODEL), BF16),
            jax.ShapeDtypeStruct((bsz, N_HEADS, VT_ROWS, t), BF16),
        ],
        compiler_params=_params(("arbitrary", "arbitrary")),
        name="kv_proj",
    )(x3, wk, wvt)


def _diff_attn_kernel(q_ref, k_ref, vt_ref, lq_ref, lk_ref, ng_ref, o_ref, m_ref, acc_ref, sa_ref, sb_ref,
                      ma_ref, mb_ref, *,
                      lambda_init):
    tq, tk = ATT_TQ, ATT_TK
    i = pl.program_id(2)
    q = q_ref[0]
    lane = lax.broadcasted_iota(jnp.int32, (tq, HEAD_W), 1)
    zero = jnp.zeros_like(q)
    qs = (jnp.where(lane < DIFF_HEAD_DIM, q, zero), jnp.where(lane >= DIFF_HEAD_DIM, q, zero))

    m_ref[...] = jnp.full(m_ref.shape, MASK_VALUE, F32)
    acc_ref[...] = jnp.zeros(acc_ref.shape, F32)

    def key_offset(n):
        t = jnp.where(n < 2, 2 * i + n, n - 2)
        return pl.multiple_of(t * tk, tk)

    def scores(n, buf, masked=False):
        s_ref, mx_ref = buf
        off = key_offset(n)
        k = k_ref[0, pl.ds(off, tk), :]
        for c in range(2):
            st = _dot_nt(k, qs[c])
            if masked:
                kpos = off + lax.broadcasted_iota(jnp.int32, (tk, tq), 0)
                qpos = i * tq + lax.broadcasted_iota(jnp.int32, (tk, tq), 1)
                st = jnp.where(kpos <= qpos, st, MASK_VALUE)
            s_ref[c] = st
            mx_ref[c] = jnp.max(st, axis=0, keepdims=True)

    def consume(n, buf):
        s_ref, mx_ref = buf
        vt = vt_ref[0, 0, :, pl.ds(key_offset(n), tk)]
        for c in range(2):
            m_old = m_ref[c]
            m_new = jnp.maximum(m_old, mx_ref[c])
            alpha = jnp.exp2(m_old - m_new)
            p = jnp.exp2(s_ref[c] - m_new).astype(BF16)
            acc_ref[c] = alpha * acc_ref[c] + _dot(vt, p)
            m_ref[c] = m_new

    buf_a = (sa_ref, ma_ref)
    buf_b = (sb_ref, mb_ref)
    scores(0, buf_a, masked=True)
    scores(1, buf_b, masked=True)
    consume(0, buf_a)
    scores(2, buf_a)
    consume(1, buf_b)

    def pair(n):
        scores(n + 1, buf_b)
        consume(n, buf_a)
        scores(n + 2, buf_a)
        consume(n + 1, buf_b)

    def body(qq, carry):
        pair(4 * qq + 2)
        pair(4 * qq + 4)
        return carry

    lax.fori_loop(0, lax.shift_right_logical(i, 1), body, 0)

    @pl.when((i & 1) == 1)
    def _():
        pair(2 * i)

    lqk = lq_ref[...] * lk_ref[...]
    lam = (jnp.exp(jnp.sum(lqk[0:1, :], axis=-1, keepdims=True))
           - jnp.exp(jnp.sum(lqk[1:2, :], axis=-1, keepdims=True)) + lambda_init)
    a1 = acc_ref[0]
    a2 = acc_ref[1]
    ot = a1[:HEAD_W] / a1[HEAD_W:HEAD_W + 1] - lam * (a2[:HEAD_W] / a2[HEAD_W:HEAD_W + 1])
    ms = jnp.mean(ot * ot, axis=0, keepdims=True)
    ot = ot * (lax.rsqrt(ms + NORM_EPS) * (1.0 - lambda_init))
    o_ref[0] = (ot.T * ng_ref[...]).astype(o_ref.dtype)


def _diff_attn(q, k, vt, lq, lk, ng, lambda_init):
    bsz, t, _ = q.shape
    tq = ATT_TQ
    return pl.pallas_call(
        functools.partial(_diff_attn_kernel, lambda_init=lambda_init),
        grid=(bsz, N_HEADS, t // tq),
        in_specs=[
            pl.BlockSpec((1, tq, HEAD_W), lambda b, h, i: (b, i, h)),
            pl.BlockSpec((1, t, HEAD_W), lambda b, h, i: (b, 0, h)),
            pl.BlockSpec((1, 1, VT_ROWS, t), lambda b, h, i: (b, h, 0, 0)),
            _const_spec(lq.shape), _const_spec(lk.shape), _const_spec(ng.shape),
        ],
        out_specs=pl.BlockSpec((1, tq, HEAD_W), lambda b, h, i: (b, i, h)),
        out_shape=jax.ShapeDtypeStruct((bsz, t, D_MODEL), BF16),
        scratch_shapes=[pltpu.VMEM((2, 1, tq), F32), pltpu.VMEM((2, VT_ROWS, tq), F32),
                        pltpu.VMEM((2, ATT_TK, tq), F32), pltpu.VMEM((2, ATT_TK, tq), F32),
                        pltpu.VMEM((2, 1, tq), F32), pltpu.VMEM((2, 1, tq), F32)],
        compiler_params=_params(("arbitrary", "arbitrary", "arbitrary")),
        name="diff_attn",
    )(q, k, vt, lq, lk, ng)


def kernel(x, ln_g, ln_b, ffn1_w_in, ffn1_w_out, ffn2_w_in, ffn2_w_out, gdn_w_in, gdn_conv_w, gdn_a_log,
           gdn_dt_bias, gdn_norm_g, gdn_w_out, diff_w_kv, diff_lambda_k, diff_w_q, diff_lambda_q, diff_norm_g,
           diff_w_out):
    bsz, t, d = x.shape
    n = bsz * t
    xs = x.reshape(n, d)
    w = 4 * D_MODEL

    def ffn(xs, w_in, w_out, g, b, mix=None):
        return _ffn_ln(xs, w_in[:, :D_FF].astype(BF16), w_in[:, D_FF:].astype(BF16), w_out.astype(BF16),
                       g.reshape(1, d), b.reshape(1, d), mix=mix)

    def pad_lanes(a, lo):
        return jnp.pad(a, [(0, 0)] * (a.ndim - 1) + [(lo, HEAD_W - lo - a.shape[-1])])

    k_sh = v_sh = None
    for l in range(DEPTH):
        xs = ffn(xs, ffn1_w_in[l], ffn1_w_out[l], ln_g[l, 0], ln_b[l, 0])
        if l < N_A:
            w_in = gdn_w_in[l]
            q, k, v, gate, beta, gc = _gdn_in(
                xs.reshape(bsz, t, d),
                w_in[:, :3 * D_MODEL].astype(BF16), w_in[:, 3 * D_MODEL:w].astype(BF16),
                pad_lanes(w_in[:, w:], 0).astype(BF16),
                gdn_conv_w[l], pad_lanes(gdn_a_log[l].reshape(1, N_HEADS), N_HEADS),
                pad_lanes(gdn_dt_bias[l].reshape(1, N_HEADS), N_HEADS))
            o = _gdn_chunk(q, k, v, gate, beta, gc, gdn_norm_g[l].reshape(1, HEAD_W))
            w_out = gdn_w_out[l]
        else:
            j = l - N_A
            lambda_init = 0.8 - 0.6 * math.exp(-0.3 * l)
            (q,) = _proj(xs, diff_w_q[j].astype(BF16), scale=DIFF_HEAD_DIM ** -0.5 * LOG2E)
            o = _diff_attn(q.reshape(bsz, t, d), k_sh, v_sh, diff_lambda_q[j], diff_lambda_k,
                           diff_norm_g[j].reshape(1, HEAD_W), lambda_init)
            w_out = diff_w_out[j]
        mix = (o.reshape(n, d), w_out.astype(BF16), ln_g[l, 1].reshape(1, d), ln_b[l, 1].reshape(1, d))
        xs = ffn(xs, ffn2_w_in[l], ffn2_w_out[l], ln_g[l, 2], ln_b[l, 2], mix=mix)
        if l == N_A - 1:
            k_sh, v_sh = _kv_proj(xs.reshape(bsz, t, d), diff_w_kv[:, :d].astype(BF16),
                                  diff_w_kv[:, d:].T.astype(BF16))
    return xs.reshape(bsz, t, d)
```

```python
import functools
import math

import jax
import jax.numpy as jnp
from jax import lax
from jax.experimental import pallas as pl
from jax.experimental.pallas import tpu as pltpu

F32 = jnp.float32
BF16 = jnp.bfloat16

D_MODEL = 1024
DEPTH = 4
N_A = DEPTH // 2
ALPHA = (2.0 * DEPTH) ** 0.25
LN_EPS = 1e-5
D_FF = 2816
HEAD_W = 128
N_HEADS = D_MODEL // HEAD_W
GDN_CONV = 4
GDN_CHUNK = 64
PAIR = 2 * GDN_CHUNK
DIFF_HEAD_DIM = 64
NORM_EPS = 1e-5
LOG2E = 1.4426950408889634

VMEM_LIMIT_V7X = 56 * 1024 * 1024

FFN_TM = 512
FFN_FC = 256
LIN_TM = 512
GDN_IN_TM = 256
GDN_TM = 512
GDN_GROUP = 2
ATT_TQ = 512
ATT_TK = ATT_TQ // 2
ATT_HEADS = 2
MASK_VALUE = -1e30


def _params(sem):
    return pltpu.CompilerParams(dimension_semantics=sem, vmem_limit_bytes=VMEM_LIMIT_V7X)


def _const_spec(shape):
    nd = len(shape)
    return pl.BlockSpec(shape, lambda *_: (0,) * nd, pipeline_mode=pl.Buffered(1))


def _layer_norm(y, g, b):
    mu = jnp.mean(y, axis=-1, keepdims=True)
    yc = y - mu
    var = jnp.mean(yc * yc, axis=-1, keepdims=True)
    return yc * lax.rsqrt(var + LN_EPS) * g + b


def _dot(a, b):
    return jnp.dot(a, b, preferred_element_type=F32)


def _dot_nt(a, b):
    return lax.dot_general(a, b, (((1,), (1,)), ((), ())), preferred_element_type=F32)


def _dot_tn(a, b):
    return lax.dot_general(a, b, (((0,), (0,)), ((), ())), preferred_element_type=F32)


def _silu(x):
    return x * jax.nn.sigmoid(x)


def _ffn_ln_kernel(*refs, with_mix):
    if with_mix:
        mix_ref, wm_ref, gm_ref, bm_ref, x_ref, wg_ref, wu_ref, wo_ref, g_ref, b_ref, o_ref, act_ref = refs
        x = _layer_norm(ALPHA * x_ref[...] + _dot(mix_ref[...], wm_ref[...]), gm_ref[...], bm_ref[...])
    else:
        x_ref, wg_ref, wu_ref, wo_ref, g_ref, b_ref, o_ref, act_ref = refs
        x = x_ref[...]
    xb = x.astype(BF16)
    for c in range(D_FF // FFN_FC):
        sl = slice(c * FFN_FC, (c + 1) * FFN_FC)
        gate = _dot(xb, wg_ref[:, sl])
        up = _dot(xb, wu_ref[:, sl])
        act_ref[:, sl] = (_silu(gate) * up).astype(BF16)
    h = _dot(act_ref[...], wo_ref[...])
    o_ref[...] = _layer_norm(ALPHA * x + 0.5 * h, g_ref[...], b_ref[...])


def _ffn_ln(x, wg, wu, wo, g, b, mix=None):
    n = x.shape[0]
    row_blk = pl.BlockSpec((FFN_TM, D_MODEL), lambda i: (i, 0))
    vec = _const_spec((1, D_MODEL))
    mix_specs = [] if mix is None else [row_blk, _const_spec((D_MODEL, D_MODEL)), vec, vec]
    return pl.pallas_call(
        functools.partial(_ffn_ln_kernel, with_mix=mix is not None),
        grid=(n // FFN_TM,),
        in_specs=mix_specs + [
            row_blk,
            _const_spec((D_MODEL, D_FF)),
            _const_spec((D_MODEL, D_FF)),
            _const_spec((D_FF, D_MODEL)),
            vec,
            vec,
        ],
        out_specs=row_blk,
        out_shape=jax.ShapeDtypeStruct((n, D_MODEL), F32),
        scratch_shapes=[pltpu.VMEM((FFN_TM, D_FF), BF16)],
        compiler_params=_params(("arbitrary",)),
        name="ffn_ln" if mix is None else "mix_ffn_ln",
    )(*(() if mix is None else mix), x, wg, wu, wo, g, b)


def _proj_kernel(x_ref, w_ref, *o_refs, scale):
    xb = x_ref[...].astype(BF16)
    for j, o_ref in enumerate(o_refs):
        y = _dot(xb, w_ref[:, j * D_MODEL:(j + 1) * D_MODEL])
        if scale != 1.0:
            y = y * scale
        o_ref[...] = y.astype(o_ref.dtype)


def _proj(x, w, scale=1.0):
    n = x.shape[0]
    n_out = w.shape[1] // D_MODEL
    outs = pl.pallas_call(
        functools.partial(_proj_kernel, scale=scale),
        grid=(n // LIN_TM,),
        in_specs=[
            pl.BlockSpec((LIN_TM, D_MODEL), lambda i: (i, 0)),
            _const_spec(w.shape),
        ],
        out_specs=[pl.BlockSpec((LIN_TM, D_MODEL), lambda i: (i, 0))] * n_out,
        out_shape=[jax.ShapeDtypeStruct((n, D_MODEL), BF16)] * n_out,
        compiler_params=_params(("arbitrary",)),
        name="proj",
    )(x, w)
    return outs


def _split3(x):
    x1 = x.astype(BF16)
    r1 = x - x1.astype(F32)
    x2 = r1.astype(BF16)
    x3 = (r1 - x2.astype(F32)).astype(BF16)
    return x1, x2, x3


def _gdn_in_kernel(x_ref, wqkv_ref, wz_ref, wba_ref, cw_ref, alog_ref, dtb_ref,
                   q_ref, k_ref, v_ref, gate_ref, beta_ref, gc_ref, ext_ref):
    tm = GDN_IN_TM
    w3 = 3 * D_MODEL

    @pl.when(pl.program_id(1) == 0)
    def _():
        ext_ref[0:8, :] = jnp.zeros((8, w3), F32)

    xb = x_ref[0].astype(BF16)

    ba = _dot(xb, wba_ref[...])
    beta = jax.nn.sigmoid(ba)
    a = ba + dtb_ref[...]
    softplus = jnp.maximum(a, 0.0) + jnp.log(1.0 + jnp.exp(-jnp.abs(a)))
    g = -jnp.exp(alog_ref[...]) * softplus
    row = lax.broadcasted_iota(jnp.int32, (tm, tm), 0)
    col = lax.broadcasted_iota(jnp.int32, (tm, tm), 1)
    tri = ((row // GDN_CHUNK == col // GDN_CHUNK) & (col <= row)).astype(BF16)
    g1, g2, g3 = _split3(g)
    gc = _dot(tri, g1) + _dot(tri, g2) + _dot(tri, g3)
    for h in range(N_HEADS):
        sl = slice(h * HEAD_W, (h + 1) * HEAD_W)
        beta_ref[0, :, sl] = jnp.broadcast_to(beta[:, h:h + 1], (tm, HEAD_W))
        gc_ref[0, :, sl] = jnp.broadcast_to(gc[:, N_HEADS + h:N_HEADS + h + 1], (tm, HEAD_W))
    gate_ref[0] = _silu(_dot(xb, wz_ref[...]))

    ext_ref[8:8 + tm, :] = _dot(xb, wqkv_ref[...])
    outs = (q_ref, k_ref, v_ref)
    for cb in range(3 * N_HEADS):
        sl = slice(cb * HEAD_W, (cb + 1) * HEAD_W)
        y = cw_ref[3:4, sl] * ext_ref[8:8 + tm, sl]
        for j in range(GDN_CONV - 1):
            y = y + cw_ref[j:j + 1, sl] * ext_ref[5 + j:5 + j + tm, sl]
        y = _silu(y)
        which, h = divmod(cb, N_HEADS)
        if which < 2:
            ss = jnp.sum(y * y, axis=-1, keepdims=True)
            y = y * lax.rsqrt(ss + 1e-6)
            if which == 0:
                y = y * (HEAD_W ** -0.5)
        outs[which][0, :, h * HEAD_W:(h + 1) * HEAD_W] = y

    ext_ref[0:8, :] = ext_ref[tm:tm + 8, :]


def _gdn_in(x3, wqkv, wz, wba, cw, alog, dtb):
    bsz, t, _ = x3.shape
    tm = GDN_IN_TM
    blk = pl.BlockSpec((1, tm, D_MODEL), lambda b, i: (b, i, 0))
    return pl.pallas_call(
        _gdn_in_kernel,
        grid=(bsz, t // tm),
        in_specs=[
            blk,
            _const_spec(wqkv.shape), _const_spec(wz.shape), _const_spec(wba.shape),
            _const_spec(cw.shape), _const_spec(alog.shape), _const_spec(dtb.shape),
        ],
        out_specs=[blk] * 6,
        out_shape=[jax.ShapeDtypeStruct((bsz, t, D_MODEL), F32)] * 6,
        scratch_shapes=[pltpu.VMEM((tm + 8, 3 * D_MODEL), F32)],
        compiler_params=_params(("arbitrary", "arbitrary")),
        name="gdn_in",
    )(x3, wqkv, wz, wba, cw, alog, dtb)


def _mm(a, b):
    return _dot(a.astype(BF16), b.astype(BF16))


def _gdn_chunk_kernel(q_ref, k_ref, v_ref, gate_ref, beta_ref, gc_ref, ng_ref, o_ref, s_ref):
    @pl.when(pl.program_id(1) == 0)
    def _():
        s_ref[...] = jnp.zeros_like(s_ref)

    row = lax.broadcasted_iota(jnp.int32, (PAIR, PAIR), 0)
    col = lax.broadcasted_iota(jnp.int32, (PAIR, PAIR), 1)
    same64 = (row // 64) == (col // 64)
    causal = same64 & (col <= row)
    strict = same64 & (col < row)
    same32 = (row // 32) == (col // 32)
    same16 = (row // 16) == (col // 16)
    m16 = (strict & same16).astype(F32)
    m32 = (strict & same32 & ~same16).astype(F32)
    m64 = (strict & ~same32).astype(F32)
    first = lax.broadcasted_iota(jnp.int32, (PAIR, HEAD_W), 0) < GDN_CHUNK
    ng = ng_ref[...]
    heads = range(N_HEADS)

    def each(fn, *lists):
        return [fn(*args) for args in zip(*lists)]

    def group_body(it, carry):
        base = it * (GDN_GROUP * PAIR)
        rows = [pl.ds(pl.multiple_of(base + g * PAIR, PAIR), PAIR) for g in range(GDN_GROUP)]
        units = [(g, h) for g in range(GDN_GROUP) for h in heads]

        def load(ref):
            return [ref[0, rows[g], h * HEAD_W:(h + 1) * HEAD_W] for g, h in units]

        q, k, v, beta, gc = load(q_ref), load(k_ref), load(v_ref), load(beta_ref), load(gc_ref)

        def decay_of(g):
            d = jnp.where(causal, g - g.T, 0.0)
            return jnp.where(causal, jnp.exp(d), 0.0)

        decay = each(decay_of, gc)
        kb = each(lambda a, b: a * b, k, beta)
        kbf = each(lambda a: a.astype(BF16), k)
        a_mat = each(lambda a, b, c: _dot_nt(a.astype(BF16), b) * c, kb, kbf, decay)
        attn = each(lambda a, b, c: _dot_nt(a.astype(BF16), b) * c, q, kbf, decay)
        egc = each(jnp.exp, gc)

        n1 = each(lambda a: -(a * m16), a_mat)
        n2 = each(_mm, n1, n1)
        x = each(lambda a, b, c: a + b + c, n1, n2, each(_mm, n1, n2))
        n4 = each(_mm, n2, n2)
        x = each(lambda a, b, c: a + b + c, x, n4, each(_mm, x, n4))
        n8 = each(_mm, n4, n4)
        x = each(lambda a, b, c: a + b + c, x, n8, each(_mm, x, n8))
        for mask in (m32, m64):
            low = each(lambda a: a * mask, a_mat)
            t = each(lambda a, b: a + b, low, each(_mm, low, x))
            x = each(lambda a, b, c: a - b - c, x, t, each(_mm, x, t))

        rhs = each(lambda a, b, c, d: jnp.concatenate([a * b, c * d], axis=1), kb, egc, v, beta)
        wu = each(lambda a, b: b + _mm(a, b), x, rhs)
        w = each(lambda a: a[:, :HEAD_W].astype(BF16), wu)
        u = each(lambda a: a[:, HEAD_W:], wu)

        gl = each(lambda g: jnp.where(first, g[GDN_CHUNK - 1:GDN_CHUNK, :], g[PAIR - 1:PAIR, :]), gc)
        kd = each(lambda a, b, c: (a * jnp.exp(b - c)).astype(BF16), k, gl, gc)
        qd = each(lambda a, b: (a * b).astype(BF16), q, egc)
        cd = each(jnp.exp, gl)
        gate = load(gate_ref)

        s = [s_ref[h] for h in heads]
        for g in range(GDN_GROUP):
            of_pair = lambda lst: lst[g * N_HEADS:(g + 1) * N_HEADS]
            u_g, w_g, qd_g, kd_g, cd_g = of_pair(u), of_pair(w), of_pair(qd), of_pair(kd), of_pair(cd)
            vns, qss = [], []
            for c in range(2):
                rs = slice(c * GDN_CHUNK, (c + 1) * GDN_CHUNK)
                cd_row = slice(c * GDN_CHUNK, c * GDN_CHUNK + 1)
                sb = each(lambda a: a.astype(BF16), s)
                vn = each(lambda a, b, c2: a[rs] - _dot(b[rs], c2), u_g, w_g, sb)
                qss.append(each(lambda a, b: _dot(a[rs], b), qd_g, sb))
                s = each(lambda a, b, c2, d: a * b[cd_row, :] + _dot_tn(c2[rs], d.astype(BF16)), s, cd_g, kd_g, vn)
                vns.append(vn)
            vn = each(lambda a, b: jnp.concatenate([a, b], axis=0), vns[0], vns[1])
            qs = each(lambda a, b: jnp.concatenate([a, b], axis=0), qss[0], qss[1])
            o = each(lambda a, b, c: a + _mm(b, c), qs, of_pair(attn), vn)
            gate_g = of_pair(gate)
            for h in heads:
                ms = jnp.mean(o[h] * o[h], axis=-1, keepdims=True)
                oh = o[h] * lax.rsqrt(ms + NORM_EPS) * ng * gate_g[h]
                o_ref[0, rows[g], h * HEAD_W:(h + 1) * HEAD_W] = oh.astype(o_ref.dtype)
        for h in heads:
            s_ref[h] = s[h]
        return carry

    lax.fori_loop(0, GDN_TM // (GDN_GROUP * PAIR), group_body, 0)


def _gdn_chunk(q, k, v, gate, beta, gc, ng):
    bsz, t, _ = q.shape
    tm = GDN_TM
    blk = pl.BlockSpec((1, tm, D_MODEL), lambda b, i: (b, i, 0))
    return pl.pallas_call(
        _gdn_chunk_kernel,
        grid=(bsz, t // tm),
        in_specs=[blk] * 6 + [_const_spec(ng.shape)],
        out_specs=blk,
        out_shape=jax.ShapeDtypeStruct((bsz, t, D_MODEL), BF16),
        scratch_shapes=[pltpu.VMEM((N_HEADS, HEAD_W, HEAD_W), F32)],
        compiler_params=_params(("arbitrary", "arbitrary")),
        name="gdn_chunk",
    )(q, k, v, gate, beta, gc, ng)


VT_ROWS = HEAD_W + 16


def _kv_proj_kernel(x_ref, wk_ref, wvt_ref, k_ref, vt_ref):
    xb = x_ref[0].astype(BF16)
    k_ref[0] = _dot(xb, wk_ref[...]).astype(BF16)
    vt = _dot_nt(wvt_ref[...], xb).astype(BF16)
    tm = vt.shape[1]
    for h in range(N_HEADS):
        vt_ref[0, h, 0:HEAD_W, :] = vt[h * HEAD_W:(h + 1) * HEAD_W, :]
        vt_ref[0, h, HEAD_W:VT_ROWS, :] = jnp.ones((VT_ROWS - HEAD_W, tm), BF16)


def _kv_proj(x3, wk, wvt):
    bsz, t, _ = x3.shape
    tm = LIN_TM
    return pl.pallas_call(
        _kv_proj_kernel,
        grid=(bsz, t // tm),
        in_specs=[
            pl.BlockSpec((1, tm, D_MODEL), lambda b, i: (b, i, 0)),
            _const_spec(wk.shape), _const_spec(wvt.shape),
        ],
        out_specs=[
            pl.BlockSpec((1, tm, D_MODEL), lambda b, i: (b, i, 0)),
            pl.BlockSpec((1, N_HEADS, VT_ROWS, tm), lambda b, i: (b, 0, 0, i)),
        ],
        out_shape=[
            jax.ShapeDtypeStruct((bsz, t, D_MODEL), BF16),
            jax.ShapeDtypeStruct((bsz, N_HEADS, VT_ROWS, t), BF16),
        ],
        compiler_params=_params(("arbitrary", "arbitrary")),
        name="kv_proj",
    )(x3, wk, wvt)


def _diff_attn_kernel(q_ref, k_ref, vt_ref, lq_ref, lk_ref, ng_ref, bias_ref, o_ref, m_ref, acc_ref, sa_ref, sb_ref,
                      ma_ref, mb_ref, *,
                      lambda_init):
    tq, tk = ATT_TQ, ATT_TK
    i = pl.program_id(2)
    lane = lax.broadcasted_iota(jnp.int32, (tq, HEAD_W), 1)
    streams = [(hh, c) for hh in range(ATT_HEADS) for c in range(2)]
    qs = []
    for hh, c in streams:
        q = q_ref[0, :, hh * HEAD_W:(hh + 1) * HEAD_W]
        keep = (lane < DIFF_HEAD_DIM) if c == 0 else (lane >= DIFF_HEAD_DIM)
        qs.append(jnp.where(keep, q, jnp.zeros_like(q)))

    m_ref[...] = jnp.full(m_ref.shape, MASK_VALUE, F32)
    acc_ref[...] = jnp.zeros(acc_ref.shape, F32)

    def key_offset(n):
        t = jnp.where(n < 2, 2 * i + n, n - 2)
        return pl.multiple_of(t * tk, tk)

    def scores(n, buf, diag=None, q0=0):
        s_ref, mx_ref = buf
        off = key_offset(n)
        ks = [k_ref[0, pl.ds(off, tk), hh * HEAD_W:(hh + 1) * HEAD_W] for hh in range(ATT_HEADS)]
        sts = [_dot_nt(ks[hh], qs[j][q0:, :]) for j, (hh, _) in enumerate(streams)]
        for j, st in enumerate(sts):
            if diag is not None:
                st = st + bias_ref[diag, :, q0:]
            s_ref[j, :, q0:] = st
            mx_ref[j, :, q0:] = jnp.max(st, axis=0, keepdims=True)

    def consume(n, buf, q0=0):
        s_ref, mx_ref = buf
        off = key_offset(n)
        vts = [vt_ref[0, hh, :, pl.ds(off, tk)] for hh in range(ATT_HEADS)]
        js = range(len(streams))
        m_old = [m_ref[j, :, q0:] for j in js]
        m_new = [jnp.maximum(m_old[j], mx_ref[j, :, q0:]) for j in js]
        alpha = [jnp.exp2(m_old[j] - m_new[j]) for j in js]
        p = [jnp.exp2(s_ref[j, :, q0:] - m_new[j]).astype(BF16) for j in js]
        pv = [_dot(vts[streams[j][0]], p[j]) for j in js]
        for j in js:
            acc_ref[j, :, q0:] = alpha[j] * acc_ref[j, :, q0:] + pv[j]
            m_ref[j, :, q0:] = m_new[j]

    buf_a = (sa_ref, ma_ref)
    buf_b = (sb_ref, mb_ref)
    scores(0, buf_a, diag=0)
    scores(1, buf_b, diag=1, q0=tk)
    consume(0, buf_a)
    scores(2, buf_a)
    consume(1, buf_b, q0=tk)

    def pair(n):
        scores(n + 1, buf_b)
        consume(n, buf_a)
        scores(n + 2, buf_a)
        consume(n + 1, buf_b)

    def body(qq, carry):
        pair(4 * qq + 2)
        pair(4 * qq + 4)
        return carry

    lax.fori_loop(0, lax.shift_right_logical(i, 1), body, 0)

    @pl.when((i & 1) == 1)
    def _():
        pair(2 * i)

    lqk = lq_ref[...] * lk_ref[...]
    lam = (jnp.exp(jnp.sum(lqk[0:1, :], axis=-1, keepdims=True))
           - jnp.exp(jnp.sum(lqk[1:2, :], axis=-1, keepdims=True)) + lambda_init)
    for hh in range(ATT_HEADS):
        a1 = acc_ref[2 * hh]
        a2 = acc_ref[2 * hh + 1]
        ot = a1[:HEAD_W] / a1[HEAD_W:HEAD_W + 1] - lam * (a2[:HEAD_W] / a2[HEAD_W:HEAD_W + 1])
        ms = jnp.mean(ot * ot, axis=0, keepdims=True)
        ot = ot * (lax.rsqrt(ms + NORM_EPS) * (1.0 - lambda_init))
        o_ref[0, :, hh * HEAD_W:(hh + 1) * HEAD_W] = (ot.T * ng_ref[...]).astype(o_ref.dtype)


def _diff_attn(q, k, vt, lq, lk, ng, lambda_init):
    bsz, t, _ = q.shape
    tq, tk = ATT_TQ, ATT_TK
    kpos = jnp.arange(tk)[None, :, None] + tk * jnp.arange(tq // tk)[:, None, None]
    bias = jnp.where(kpos <= jnp.arange(tq)[None, None, :], 0.0, MASK_VALUE).astype(F32)
    ns = 2 * ATT_HEADS
    return pl.pallas_call(
        functools.partial(_diff_attn_kernel, lambda_init=lambda_init),
        grid=(bsz, N_HEADS // ATT_HEADS, t // tq),
        in_specs=[
            pl.BlockSpec((1, tq, ATT_HEADS * HEAD_W), lambda b, h, i: (b, i, h)),
            pl.BlockSpec((1, t, ATT_HEADS * HEAD_W), lambda b, h, i: (b, 0, h)),
            pl.BlockSpec((1, ATT_HEADS, VT_ROWS, t), lambda b, h, i: (b, h, 0, 0)),
            _const_spec(lq.shape), _const_spec(lk.shape), _const_spec(ng.shape), _const_spec(bias.shape),
        ],
        out_specs=pl.BlockSpec((1, tq, ATT_HEADS * HEAD_W), lambda b, h, i: (b, i, h)),
        out_shape=jax.ShapeDtypeStruct((bsz, t, D_MODEL), BF16),
        scratch_shapes=[pltpu.VMEM((ns, 1, tq), F32), pltpu.VMEM((ns, VT_ROWS, tq), F32),
                        pltpu.VMEM((ns, ATT_TK, tq), F32), pltpu.VMEM((ns, ATT_TK, tq), F32),
                        pltpu.VMEM((ns, 1, tq), F32), pltpu.VMEM((ns, 1, tq), F32)],
        compiler_params=_params(("arbitrary", "arbitrary", "arbitrary")),
        name="diff_attn",
    )(q, k, vt, lq, lk, ng, bias)


def kernel(x, ln_g, ln_b, ffn1_w_in, ffn1_w_out, ffn2_w_in, ffn2_w_out, gdn_w_in, gdn_conv_w, gdn_a_log,
           gdn_dt_bias, gdn_norm_g, gdn_w_out, diff_w_kv, diff_lambda_k, diff_w_q, diff_lambda_q, diff_norm_g,
           diff_w_out):
    bsz, t, d = x.shape
    n = bsz * t
    xs = x.reshape(n, d)
    w = 4 * D_MODEL

    def ffn(xs, w_in, w_out, g, b, mix=None):
        return _ffn_ln(xs, w_in[:, :D_FF].astype(BF16), w_in[:, D_FF:].astype(BF16), w_out.astype(BF16),
                       g.reshape(1, d), b.reshape(1, d), mix=mix)

    def pad_lanes(a, lo):
        return jnp.pad(a, [(0, 0)] * (a.ndim - 1) + [(lo, HEAD_W - lo - a.shape[-1])])

    k_sh = v_sh = None
    for l in range(DEPTH):
        xs = ffn(xs, ffn1_w_in[l], ffn1_w_out[l], ln_g[l, 0], ln_b[l, 0])
        if l < N_A:
            w_in = gdn_w_in[l]
            q, k, v, gate, beta, gc = _gdn_in(
                xs.reshape(bsz, t, d),
                w_in[:, :3 * D_MODEL].astype(BF16), w_in[:, 3 * D_MODEL:w].astype(BF16),
                pad_lanes(w_in[:, w:], 0).astype(BF16),
                gdn_conv_w[l], pad_lanes(gdn_a_log[l].reshape(1, N_HEADS), N_HEADS),
                pad_lanes(gdn_dt_bias[l].reshape(1, N_HEADS), N_HEADS))
            o = _gdn_chunk(q, k, v, gate, beta, gc, gdn_norm_g[l].reshape(1, HEAD_W))
            w_out = gdn_w_out[l]
        else:
            j = l - N_A
            lambda_init = 0.8 - 0.6 * math.exp(-0.3 * l)
            (q,) = _proj(xs, diff_w_q[j].astype(BF16), scale=DIFF_HEAD_DIM ** -0.5 * LOG2E)
            o = _diff_attn(q.reshape(bsz, t, d), k_sh, v_sh, diff_lambda_q[j], diff_lambda_k,
                           diff_norm_g[j].reshape(1, HEAD_W), lambda_init)
            w_out = diff_w_out[j]
        mix = (o.reshape(n, d), w_out.astype(BF16), ln_g[l, 1].reshape(1, d), ln_b[l, 1].reshape(1, d))
        xs = ffn(xs, ffn2_w_in[l], ffn2_w_out[l], ln_g[l, 2], ln_b[l, 2], mix=mix)
        if l == N_A - 1:
            k_sh, v_sh = _kv_proj(xs.reshape(bsz, t, d), diff_w_kv[:, :d].astype(BF16),
                                  diff_w_kv[:, d:].T.astype(BF16))
    return xs.reshape(bsz, t, d)
```

```python
import functools
import math

import jax
import jax.numpy as jnp
from jax import lax
from jax.experimental import pallas as pl
from jax.experimental.pallas import tpu as pltpu

F32 = jnp.float32
BF16 = jnp.bfloat16

D_MODEL = 1024
DEPTH = 4
N_A = DEPTH // 2
ALPHA = (2.0 * DEPTH) ** 0.25
LN_EPS = 1e-5
D_FF = 2816
HEAD_W = 128
N_HEADS = D_MODEL // HEAD_W
GDN_CONV = 4
GDN_CHUNK = 64
PAIR = 2 * GDN_CHUNK
DIFF_HEAD_DIM = 64
NORM_EPS = 1e-5
LOG2E = 1.4426950408889634

VMEM_LIMIT_V7X = 56 * 1024 * 1024

FFN_TM = 512
FFN_FC = 256
LIN_TM = 512
GDN_IN_TM = 256
GDN_TM = 512
GDN_GROUP = 2
ATT_TQ = 512
ATT_TK = ATT_TQ // 2
ATT_HEADS = 2
MASK_VALUE = -1e30


def _params(sem):
    return pltpu.CompilerParams(dimension_semantics=sem, vmem_limit_bytes=VMEM_LIMIT_V7X)


def _const_spec(shape):
    nd = len(shape)
    return pl.BlockSpec(shape, lambda *_: (0,) * nd, pipeline_mode=pl.Buffered(1))


def _layer_norm(y, g, b):
    mu = jnp.mean(y, axis=-1, keepdims=True)
    yc = y - mu
    var = jnp.mean(yc * yc, axis=-1, keepdims=True)
    return yc * lax.rsqrt(var + LN_EPS) * g + b


def _dot(a, b):
    return jnp.dot(a, b, preferred_element_type=F32)


def _dot_nt(a, b):
    return lax.dot_general(a, b, (((1,), (1,)), ((), ())), preferred_element_type=F32)


def _dot_tn(a, b):
    return lax.dot_general(a, b, (((0,), (0,)), ((), ())), preferred_element_type=F32)


def _silu(x):
    return x * jax.nn.sigmoid(x)


def _ffn_ln_kernel(*refs, with_mix):
    if with_mix:
        mix_ref, wm_ref, gm_ref, bm_ref, x_ref, wg_ref, wu_ref, wo_ref, g_ref, b_ref, o_ref, act_ref = refs
        x = _layer_norm(ALPHA * x_ref[...] + _dot(mix_ref[...], wm_ref[...]), gm_ref[...], bm_ref[...])
    else:
        x_ref, wg_ref, wu_ref, wo_ref, g_ref, b_ref, o_ref, act_ref = refs
        x = x_ref[...]
    xb = x.astype(BF16)
    for c in range(D_FF // FFN_FC):
        sl = slice(c * FFN_FC, (c + 1) * FFN_FC)
        gate = _dot(xb, wg_ref[:, sl])
        up = _dot(xb, wu_ref[:, sl])
        act_ref[:, sl] = (_silu(gate) * up).astype(BF16)
    h = _dot(act_ref[...], wo_ref[...])
    o_ref[...] = _layer_norm(ALPHA * x + 0.5 * h, g_ref[...], b_ref[...])


def _ffn_ln(x, wg, wu, wo, g, b, mix=None):
    n = x.shape[0]
    row_blk = pl.BlockSpec((FFN_TM, D_MODEL), lambda i: (i, 0))
    vec = _const_spec((1, D_MODEL))
    mix_specs = [] if mix is None else [row_blk, _const_spec((D_MODEL, D_MODEL)), vec, vec]
    return pl.pallas_call(
        functools.partial(_ffn_ln_kernel, with_mix=mix is not None),
        grid=(n // FFN_TM,),
        in_specs=mix_specs + [
            row_blk,
            _const_spec((D_MODEL, D_FF)),
            _const_spec((D_MODEL, D_FF)),
            _const_spec((D_FF, D_MODEL)),
            vec,
            vec,
        ],
        out_specs=row_blk,
        out_shape=jax.ShapeDtypeStruct((n, D_MODEL), F32),
        scratch_shapes=[pltpu.VMEM((FFN_TM, D_FF), BF16)],
        compiler_params=_params(("arbitrary",)),
        name="ffn_ln" if mix is None else "mix_ffn_ln",
    )(*(() if mix is None else mix), x, wg, wu, wo, g, b)


def _proj_kernel(x_ref, w_ref, *o_refs, scale):
    xb = x_ref[...].astype(BF16)
    for j, o_ref in enumerate(o_refs):
        y = _dot(xb, w_ref[:, j * D_MODEL:(j + 1) * D_MODEL])
        if scale != 1.0:
            y = y * scale
        o_ref[...] = y.astype(o_ref.dtype)


def _proj(x, w, scale=1.0):
    n = x.shape[0]
    n_out = w.shape[1] // D_MODEL
    outs = pl.pallas_call(
        functools.partial(_proj_kernel, scale=scale),
        grid=(n // LIN_TM,),
        in_specs=[
            pl.BlockSpec((LIN_TM, D_MODEL), lambda i: (i, 0)),
            _const_spec(w.shape),
        ],
        out_specs=[pl.BlockSpec((LIN_TM, D_MODEL), lambda i: (i, 0))] * n_out,
        out_shape=[jax.ShapeDtypeStruct((n, D_MODEL), BF16)] * n_out,
        compiler_params=_params(("arbitrary",)),
        name="proj",
    )(x, w)
    return outs


def _split3(x):
    x1 = x.astype(BF16)
    r1 = x - x1.astype(F32)
    x2 = r1.astype(BF16)
    x3 = (r1 - x2.astype(F32)).astype(BF16)
    return x1, x2, x3


def _gdn_in_kernel(x_ref, wqkv_ref, wz_ref, wba_ref, cw_ref, alog_ref, dtb_ref,
                   q_ref, k_ref, v_ref, gate_ref, beta_ref, gc_ref, ext_ref):
    tm = GDN_IN_TM
    w3 = 3 * D_MODEL

    @pl.when(pl.program_id(1) == 0)
    def _():
        ext_ref[0:8, :] = jnp.zeros((8, w3), F32)

    xb = x_ref[0].astype(BF16)

    ba = _dot(xb, wba_ref[...])
    beta = jax.nn.sigmoid(ba)
    a = ba + dtb_ref[...]
    softplus = jnp.maximum(a, 0.0) + jnp.log(1.0 + jnp.exp(-jnp.abs(a)))
    g = -jnp.exp(alog_ref[...]) * softplus
    row = lax.broadcasted_iota(jnp.int32, (tm, tm), 0)
    col = lax.broadcasted_iota(jnp.int32, (tm, tm), 1)
    tri = ((row // PAIR == col // PAIR) & (col <= row)).astype(BF16)
    g1, g2, g3 = _split3(g)
    gc = _dot(tri, g1) + _dot(tri, g2) + _dot(tri, g3)
    for h in range(N_HEADS):
        sl = slice(h * HEAD_W, (h + 1) * HEAD_W)
        beta_ref[0, :, sl] = jnp.broadcast_to(beta[:, h:h + 1], (tm, HEAD_W))
        gc_ref[0, :, sl] = jnp.broadcast_to(gc[:, N_HEADS + h:N_HEADS + h + 1], (tm, HEAD_W))
    gate_ref[0] = _silu(_dot(xb, wz_ref[...]))

    ext_ref[8:8 + tm, :] = _dot(xb, wqkv_ref[...])
    outs = (q_ref, k_ref, v_ref)
    for cb in range(3 * N_HEADS):
        sl = slice(cb * HEAD_W, (cb + 1) * HEAD_W)
        y = cw_ref[3:4, sl] * ext_ref[8:8 + tm, sl]
        for j in range(GDN_CONV - 1):
            y = y + cw_ref[j:j + 1, sl] * ext_ref[5 + j:5 + j + tm, sl]
        y = _silu(y)
        which, h = divmod(cb, N_HEADS)
        if which < 2:
            ss = jnp.sum(y * y, axis=-1, keepdims=True)
            y = y * lax.rsqrt(ss + 1e-6)
            if which == 0:
                y = y * (HEAD_W ** -0.5)
        outs[which][0, :, h * HEAD_W:(h + 1) * HEAD_W] = y

    ext_ref[0:8, :] = ext_ref[tm:tm + 8, :]


def _gdn_in(x3, wqkv, wz, wba, cw, alog, dtb):
    bsz, t, _ = x3.shape
    tm = GDN_IN_TM
    blk = pl.BlockSpec((1, tm, D_MODEL), lambda b, i: (b, i, 0))
    return pl.pallas_call(
        _gdn_in_kernel,
        grid=(bsz, t // tm),
        in_specs=[
            blk,
            _const_spec(wqkv.shape), _const_spec(wz.shape), _const_spec(wba.shape),
            _const_spec(cw.shape), _const_spec(alog.shape), _const_spec(dtb.shape),
        ],
        out_specs=[blk] * 6,
        out_shape=[jax.ShapeDtypeStruct((bsz, t, D_MODEL), F32)] * 6,
        scratch_shapes=[pltpu.VMEM((tm + 8, 3 * D_MODEL), F32)],
        compiler_params=_params(("arbitrary", "arbitrary")),
        name="gdn_in",
    )(x3, wqkv, wz, wba, cw, alog, dtb)


def _mm(a, b):
    return _dot(a.astype(BF16), b.astype(BF16))


def _gdn_chunk_kernel(q_ref, k_ref, v_ref, gate_ref, beta_ref, gc_ref, ng_ref, o_ref, s_ref):
    @pl.when(pl.program_id(1) == 0)
    def _():
        s_ref[...] = jnp.zeros_like(s_ref)

    row = lax.broadcasted_iota(jnp.int32, (PAIR, PAIR), 0)
    col = lax.broadcasted_iota(jnp.int32, (PAIR, PAIR), 1)
    causal = col <= row
    strict = col < row
    same64 = (row // 64) == (col // 64)
    same32 = (row // 32) == (col // 32)
    same16 = (row // 16) == (col // 16)
    m16 = (strict & same16).astype(F32)
    m32 = (strict & same32 & ~same16).astype(F32)
    m64 = (strict & same64 & ~same32).astype(F32)
    m128 = (strict & ~same64).astype(F32)
    ng = ng_ref[...]
    heads = range(N_HEADS)

    def each(fn, *lists):
        return [fn(*args) for args in zip(*lists)]

    def group_body(it, carry):
        base = it * (GDN_GROUP * PAIR)
        rows = [pl.ds(pl.multiple_of(base + g * PAIR, PAIR), PAIR) for g in range(GDN_GROUP)]
        units = [(g, h) for g in range(GDN_GROUP) for h in heads]

        def load(ref):
            return [ref[0, rows[g], h * HEAD_W:(h + 1) * HEAD_W] for g, h in units]

        q, k, v, beta, gc = load(q_ref), load(k_ref), load(v_ref), load(beta_ref), load(gc_ref)

        def decay_of(g):
            d = jnp.where(causal, g - g.T, 0.0)
            return jnp.where(causal, jnp.exp(d), 0.0)

        decay = each(decay_of, gc)
        kb = each(lambda a, b: a * b, k, beta)
        kbf = each(lambda a: a.astype(BF16), k)
        kq = each(lambda a, b, c: _dot_nt(jnp.concatenate([a.astype(BF16), b.astype(BF16)], axis=0), c), kb, q, kbf)
        a_mat = each(lambda a, c: a[:PAIR] * c, kq, decay)
        attn = each(lambda a, c: a[PAIR:] * c, kq, decay)
        egc = each(jnp.exp, gc)

        n1 = each(lambda a: -(a * m16), a_mat)
        n2 = each(_mm, n1, n1)
        x = each(lambda a, b, c: a + b + c, n1, n2, each(_mm, n1, n2))
        n4 = each(_mm, n2, n2)
        x = each(lambda a, b, c: a + b + c, x, n4, each(_mm, x, n4))
        n8 = each(_mm, n4, n4)
        x = each(lambda a, b, c: a + b + c, x, n8, each(_mm, x, n8))
        for mask in (m32, m64, m128):
            low = each(lambda a: a * mask, a_mat)
            t = each(lambda a, b: a + b, low, each(_mm, low, x))
            x = each(lambda a, b, c: a - b - c, x, t, each(_mm, x, t))

        rhs = each(lambda a, b, c, d: jnp.concatenate([a * b, c * d], axis=1), kb, egc, v, beta)
        wu = each(lambda a, b: b + _mm(a, b), x, rhs)
        w = each(lambda a: a[:, :HEAD_W].astype(BF16), wu)
        u = each(lambda a: a[:, HEAD_W:], wu)

        gl = each(lambda g: jnp.broadcast_to(g[PAIR - 1:PAIR, :], g.shape), gc)
        kd = each(lambda a, b, c: (a * jnp.exp(b - c)).astype(BF16), k, gl, gc)
        qd = each(lambda a, b: (a * b).astype(BF16), q, egc)
        cd = each(jnp.exp, gl)
        gate = load(gate_ref)

        s = [s_ref[h] for h in heads]
        for g in range(GDN_GROUP):
            of_pair = lambda lst: lst[g * N_HEADS:(g + 1) * N_HEADS]
            u_g, w_g, qd_g, kd_g, cd_g = of_pair(u), of_pair(w), of_pair(qd), of_pair(kd), of_pair(cd)
            sb = each(lambda a: a.astype(BF16), s)
            vn = each(lambda a, b, c2: a - _dot(b, c2), u_g, w_g, sb)
            qs = each(_dot, qd_g, sb)
            s = each(lambda a, b, c2, d: a * b[0:1, :] + _dot_tn(c2, d.astype(BF16)), s, cd_g, kd_g, vn)
            o = each(lambda a, b, c: a + _mm(b, c), qs, of_pair(attn), vn)
            gate_g = of_pair(gate)
            for h in heads:
                ms = jnp.mean(o[h] * o[h], axis=-1, keepdims=True)
                oh = o[h] * lax.rsqrt(ms + NORM_EPS) * ng * gate_g[h]
                o_ref[0, rows[g], h * HEAD_W:(h + 1) * HEAD_W] = oh.astype(o_ref.dtype)
        for h in heads:
            s_ref[h] = s[h]
        return carry

    lax.fori_loop(0, GDN_TM // (GDN_GROUP * PAIR), group_body, 0)


def _gdn_chunk(q, k, v, gate, beta, gc, ng):
    bsz, t, _ = q.shape
    tm = GDN_TM
    blk = pl.BlockSpec((1, tm, D_MODEL), lambda b, i: (b, i, 0))
    return pl.pallas_call(
        _gdn_chunk_kernel,
        grid=(bsz, t // tm),
        in_specs=[blk] * 6 + [_const_spec(ng.shape)],
        out_specs=blk,
        out_shape=jax.ShapeDtypeStruct((bsz, t, D_MODEL), BF16),
        scratch_shapes=[pltpu.VMEM((N_HEADS, HEAD_W, HEAD_W), F32)],
        compiler_params=_params(("arbitrary", "arbitrary")),
        name="gdn_chunk",
    )(q, k, v, gate, beta, gc, ng)


VT_ROWS = HEAD_W + 16


def _kv_proj_kernel(x_ref, wk_ref, wvt_ref, k_ref, vt_ref):
    xb = x_ref[0].astype(BF16)
    k_ref[0] = _dot(xb, wk_ref[...]).astype(BF16)
    vt = _dot_nt(wvt_ref[...], xb).astype(BF16)
    tm = vt.shape[1]
    for h in range(N_HEADS):
        vt_ref[0, h, 0:HEAD_W, :] = vt[h * HEAD_W:(h + 1) * HEAD_W, :]
        vt_ref[0, h, HEAD_W:VT_ROWS, :] = jnp.ones((VT_ROWS - HEAD_W, tm), BF16)


def _kv_proj(x3, wk, wvt):
    bsz, t, _ = x3.shape
    tm = LIN_TM
    return pl.pallas_call(
        _kv_proj_kernel,
        grid=(bsz, t // tm),
        in_specs=[
            pl.BlockSpec((1, tm, D_MODEL), lambda b, i: (b, i, 0)),
            _const_spec(wk.shape), _const_spec(wvt.shape),
        ],
        out_specs=[
            pl.BlockSpec((1, tm, D_MODEL), lambda b, i: (b, i, 0)),
            pl.BlockSpec((1, N_HEADS, VT_ROWS, tm), lambda b, i: (b, 0, 0, i)),
        ],
        out_shape=[
            jax.ShapeDtypeStruct((bsz, t, D_MODEL), BF16),
            jax.ShapeDtypeStruct((bsz, N_HEADS, VT_ROWS, t), BF16),
        ],
        compiler_params=_params(("arbitrary", "arbitrary")),
        name="kv_proj",
    )(x3, wk, wvt)


def _diff_attn_kernel(q_ref, k_ref, vt_ref, lq_ref, lk_ref, ng_ref, bias_ref, o_ref, m_ref, acc_ref, sa_ref, sb_ref,
                      ma_ref, mb_ref, *,
                      lambda_init):
    tq, tk = ATT_TQ, ATT_TK
    i = pl.program_id(2)
    lane = lax.broadcasted_iota(jnp.int32, (tq, HEAD_W), 1)
    streams = [(hh, c) for hh in range(ATT_HEADS) for c in range(2)]
    qs = []
    for hh, c in streams:
        q = q_ref[0, :, hh * HEAD_W:(hh + 1) * HEAD_W]
        keep = (lane < DIFF_HEAD_DIM) if c == 0 else (lane >= DIFF_HEAD_DIM)
        qs.append(jnp.where(keep, q, jnp.zeros_like(q)))

    m_ref[...] = jnp.full(m_ref.shape, MASK_VALUE, F32)
    acc_ref[...] = jnp.zeros(acc_ref.shape, F32)

    def key_offset(n):
        t = jnp.where(n < 2, 2 * i + n, n - 2)
        return pl.multiple_of(t * tk, tk)

    def scores(n, buf, diag=None, q0=0):
        s_ref, mx_ref = buf
        off = key_offset(n)
        ks = [k_ref[0, pl.ds(off, tk), hh * HEAD_W:(hh + 1) * HEAD_W] for hh in range(ATT_HEADS)]
        sts = [_dot_nt(ks[hh], qs[j][q0:, :]) for j, (hh, _) in enumerate(streams)]
        for j, st in enumerate(sts):
            if diag is not None:
                st = st + bias_ref[diag, :, q0:]
            s_ref[j, :, q0:] = st
            mx_ref[j, :, q0:] = jnp.max(st, axis=0, keepdims=True)

    def consume(n, buf, q0=0):
        s_ref, mx_ref = buf
        off = key_offset(n)
        vts = [vt_ref[0, hh, :, pl.ds(off, tk)] for hh in range(ATT_HEADS)]
        js = range(len(streams))
        m_old = [m_ref[j, :, q0:] for j in js]
        m_new = [jnp.maximum(m_old[j], mx_ref[j, :, q0:]) for j in js]
        alpha = [jnp.exp2(m_old[j] - m_new[j]) for j in js]
        p = [jnp.exp2(s_ref[j, :, q0:] - m_new[j]).astype(BF16) for j in js]
        pv = [_dot(vts[streams[j][0]], p[j]) for j in js]
        for j in js:
            acc_ref[j, :, q0:] = alpha[j] * acc_ref[j, :, q0:] + pv[j]
            m_ref[j, :, q0:] = m_new[j]

    buf_a = (sa_ref, ma_ref)
    buf_b = (sb_ref, mb_ref)
    scores(0, buf_a, diag=0)
    scores(1, buf_b, diag=1, q0=tk)
    consume(0, buf_a)
    scores(2, buf_a)
    consume(1, buf_b, q0=tk)

    def pair(n):
        scores(n + 1, buf_b)
        consume(n, buf_a)
        scores(n + 2, buf_a)
        consume(n + 1, buf_b)

    def body(qq, carry):
        pair(4 * qq + 2)
        pair(4 * qq + 4)
        return carry

    lax.fori_loop(0, lax.shift_right_logical(i, 1), body, 0)

    @pl.when((i & 1) == 1)
    def _():
        pair(2 * i)

    lqk = lq_ref[...] * lk_ref[...]
    lam = (jnp.exp(jnp.sum(lqk[0:1, :], axis=-1, keepdims=True))
           - jnp.exp(jnp.sum(lqk[1:2, :], axis=-1, keepdims=True)) + lambda_init)
    for hh in range(ATT_HEADS):
        a1 = acc_ref[2 * hh]
        a2 = acc_ref[2 * hh + 1]
        ot = a1[:HEAD_W] / a1[HEAD_W:HEAD_W + 1] - lam * (a2[:HEAD_W] / a2[HEAD_W:HEAD_W + 1])
        ms = jnp.mean(ot * ot, axis=0, keepdims=True)
        ot = ot * (lax.rsqrt(ms + NORM_EPS) * (1.0 - lambda_init))
        o_ref[0, :, hh * HEAD_W:(hh + 1) * HEAD_W] = (ot.T * ng_ref[...]).astype(o_ref.dtype)


def _diff_attn(q, k, vt, lq, lk, ng, lambda_init):
    bsz, t, _ = q.shape
    tq, tk = ATT_TQ, ATT_TK
    kpos = jnp.arange(tk)[None, :, None] + tk * jnp.arange(tq // tk)[:, None, None]
    bias = jnp.where(kpos <= jnp.arange(tq)[None, None, :], 0.0, MASK_VALUE).astype(F32)
    ns = 2 * ATT_HEADS
    return pl.pallas_call(
        functools.partial(_diff_attn_kernel, lambda_init=lambda_init),
        grid=(bsz, N_HEADS // ATT_HEADS, t // tq),
        in_specs=[
            pl.BlockSpec((1, tq, ATT_HEADS * HEAD_W), lambda b, h, i: (b, i, h)),
            pl.BlockSpec((1, t, ATT_HEADS * HEAD_W), lambda b, h, i: (b, 0, h)),
            pl.BlockSpec((1, ATT_HEADS, VT_ROWS, t), lambda b, h, i: (b, h, 0, 0)),
            _const_spec(lq.shape), _const_spec(lk.shape), _const_spec(ng.shape), _const_spec(bias.shape),
        ],
        out_specs=pl.BlockSpec((1, tq, ATT_HEADS * HEAD_W), lambda b, h, i: (b, i, h)),
        out_shape=jax.ShapeDtypeStruct((bsz, t, D_MODEL), BF16),
        scratch_shapes=[pltpu.VMEM((ns, 1, tq), F32), pltpu.VMEM((ns, VT_ROWS, tq), F32),
                        pltpu.VMEM((ns, ATT_TK, tq), F32), pltpu.VMEM((ns, ATT_TK, tq), F32),
                        pltpu.VMEM((ns, 1, tq), F32), pltpu.VMEM((ns, 1, tq), F32)],
        compiler_params=_params(("arbitrary", "arbitrary", "arbitrary")),
        name="diff_attn",
    )(q, k, vt, lq, lk, ng, bias)


def kernel(x, ln_g, ln_b, ffn1_w_in, ffn1_w_out, ffn2_w_in, ffn2_w_out, gdn_w_in, gdn_conv_w, gdn_a_log,
           gdn_dt_bias, gdn_norm_g, gdn_w_out, diff_w_kv, diff_lambda_k, diff_w_q, diff_lambda_q, diff_norm_g,
           diff_w_out):
    bsz, t, d = x.shape
    n = bsz * t
    xs = x.reshape(n, d)
    w = 4 * D_MODEL

    def ffn(xs, w_in, w_out, g, b, mix=None):
        return _ffn_ln(xs, w_in[:, :D_FF].astype(BF16), w_in[:, D_FF:].astype(BF16), w_out.astype(BF16),
                       g.reshape(1, d), b.reshape(1, d), mix=mix)

    def pad_lanes(a, lo):
        return jnp.pad(a, [(0, 0)] * (a.ndim - 1) + [(lo, HEAD_W - lo - a.shape[-1])])

    k_sh = v_sh = None
    for l in range(DEPTH):
        xs = ffn(xs, ffn1_w_in[l], ffn1_w_out[l], ln_g[l, 0], ln_b[l, 0])
        if l < N_A:
            w_in = gdn_w_in[l]
            q, k, v, gate, beta, gc = _gdn_in(
                xs.reshape(bsz, t, d),
                w_in[:, :3 * D_MODEL].astype(BF16), w_in[:, 3 * D_MODEL:w].astype(BF16),
                pad_lanes(w_in[:, w:], 0).astype(BF16),
                gdn_conv_w[l], pad_lanes(gdn_a_log[l].reshape(1, N_HEADS), N_HEADS),
                pad_lanes(gdn_dt_bias[l].reshape(1, N_HEADS), N_HEADS))
            o = _gdn_chunk(q, k, v, gate, beta, gc, gdn_norm_g[l].reshape(1, HEAD_W))
            w_out = gdn_w_out[l]
        else:
            j = l - N_A
            lambda_init = 0.8 - 0.6 * math.exp(-0.3 * l)
            (q,) = _proj(xs, diff_w_q[j].astype(BF16), scale=DIFF_HEAD_DIM ** -0.5 * LOG2E)
            o = _diff_attn(q.reshape(bsz, t, d), k_sh, v_sh, diff_lambda_q[j], diff_lambda_k,
                           diff_norm_g[j].reshape(1, HEAD_W), lambda_init)
            w_out = diff_w_out[j]
        mix = (o.reshape(n, d), w_out.astype(BF16), ln_g[l, 1].reshape(1, d), ln_b[l, 1].reshape(1, d))
        xs = ffn(xs, ffn2_w_in[l], ffn2_w_out[l], ln_g[l, 2], ln_b[l, 2], mix=mix)
        if l == N_A - 1:
            k_sh, v_sh = _kv_proj(xs.reshape(bsz, t, d), diff_w_kv[:, :d].astype(BF16),
                                  diff_w_kv[:, d:].T.astype(BF16))
    return xs.reshape(bsz, t, d)
```
